```python
import jax, jax.numpy as jnp
from jax import lax
import numpy as np

D_MODEL = 4096
BATCH = 8
SEQ = 2048
DEPTH = 2

M_WIDTH = 4096
M_HEADS = 8
M_HEAD_DIM = M_WIDTH // M_HEADS
M_QKV_BLOCK = 4
M_CONV = 4
M_CHUNK = 64
A_HEADS = 32
A_HEAD_DIM = 128
A_WIDTH = A_HEADS * A_HEAD_DIM
A_LATENT = 512
IDX_HEADS = 32
IDX_DIM = 128
IDX_TOPK = 256
Q_BLOCK = 128
EPS = 1e-6

SPLITS = (M_WIDTH, M_WIDTH, M_WIDTH, M_HEADS, M_HEADS,
          A_WIDTH, A_LATENT, A_WIDTH,
          IDX_HEADS * IDX_DIM, IDX_DIM, IDX_HEADS,
          D_MODEL, D_MODEL)
N_IN = sum(SPLITS)

kernel_name = "hybrid_mlstm_dsa_gated_block"


def rmsnorm(x, g):
    x32 = x.astype(jnp.float32)
    y = x32 * lax.rsqrt(jnp.mean(x32 * x32, axis=-1, keepdims=True) + EPS)
    return (y * g.astype(jnp.float32)).astype(x.dtype)


def layernorm(x, g, b):
    x32 = x.astype(jnp.float32)
    mu = jnp.mean(x32, axis=-1, keepdims=True)
    xc = x32 - mu
    y = xc * lax.rsqrt(jnp.mean(xc * xc, axis=-1, keepdims=True) + EPS)
    return y * g.astype(jnp.float32) + b.astype(jnp.float32)


def head_layernorm(h, g):
    mu = jnp.mean(h, axis=-1, keepdims=True)
    hc = h - mu
    y = hc * lax.rsqrt(jnp.mean(hc * hc, axis=-1, keepdims=True) + EPS)
    return y * g.astype(jnp.float32).reshape(M_HEADS, M_HEAD_DIM)


def causal_dwconv(x, w, b):
    y = lax.conv_general_dilated(x, w[:, None, :].astype(x.dtype), window_strides=(1,),
                                 padding=[(M_CONV - 1, 0)],
                                 dimension_numbers=('NWC', 'WIO', 'NWC'),
                                 feature_group_count=x.shape[-1])
    return y + b.astype(x.dtype)


def blockdiag(x, w):
    xs = x.reshape(*x.shape[:-1], w.shape[0], w.shape[1])
    return jnp.einsum('...gi,gio->...go', xs, w).reshape(x.shape)


def mlstm_chunkwise(q, k, v, i_pre, f_pre):
    B, S, H, d = q.shape
    nc = S // M_CHUNK
    f32 = jnp.float32

    def to_chunks(a):
        a = a.astype(f32).reshape(B, nc, M_CHUNK, H, *a.shape[3:])
        return jnp.moveaxis(a, (1, 3), (0, 2))

    qc = to_chunks(q)
    kc = to_chunks(k) * (d ** -0.5)
    vc = to_chunks(v)
    ic = to_chunks(i_pre)
    fc = to_chunks(jax.nn.log_sigmoid(f_pre.astype(f32)))
    tril = jnp.tril(jnp.ones((M_CHUNK, M_CHUNK), dtype=bool))

    def step(carry, inp):
        C, n, m = carry
        qb, kb, vb, ib, fb = inp
        b = jnp.cumsum(fb, axis=-1)
        dmat = jnp.where(tril, b[..., :, None] - b[..., None, :] + ib[..., None, :], -jnp.inf)
        m_inter = b + m[..., None]
        m_t = jnp.maximum(m_inter, jnp.max(dmat, axis=-1))
        a = jnp.einsum('bhtd,bhsd->bhts', qb, kb) * jnp.exp(dmat - m_t[..., None])
        inter = jnp.exp(m_inter - m_t)
        num = jnp.einsum('bhts,bhsd->bhtd', a, vb) + inter[..., None] * jnp.einsum('bhtd,bhde->bhte', qb, C)
        den = jnp.sum(a, axis=-1) + inter * jnp.einsum('bhtd,bhd->bht', qb, n)
        h = num / jnp.maximum(jnp.abs(den), jnp.exp(-m_t))[..., None]
        b_last = b[..., -1]
        g = b_last[..., None] - b + ib
        m_new = jnp.maximum(b_last + m, jnp.max(g, axis=-1))
        w = jnp.exp(g - m_new[..., None])
        decay = jnp.exp(b_last + m - m_new)
        C_new = decay[..., None, None] * C + jnp.einsum('bhsd,bhse->bhde', kb * w[..., None], vb)
        n_new = decay[..., None] * n + jnp.einsum('bhs,bhsd->bhd', w, kb)
        return (C_new, n_new, m_new), h

    init = (jnp.zeros((B, H, d, d), f32), jnp.zeros((B, H, d), f32), jnp.zeros((B, H), f32))
    _, hs = lax.scan(step, init, (qc, kc, vc, ic, fc))
    return jnp.moveaxis(hs, (0, 2), (1, 3)).reshape(B, S, H, d)


def mlstm_branch(xm, om, zm, ip, fp, conv_w, conv_b, wq, wk, wv, b_i, b_f, g_head, skip):
    B, S, _ = xm.shape
    xc = jax.nn.silu(causal_dwconv(xm, conv_w, conv_b))
    q = blockdiag(xc, wq).reshape(B, S, M_HEADS, M_HEAD_DIM)
    k = blockdiag(xc, wk).reshape(B, S, M_HEADS, M_HEAD_DIM)
    v = blockdiag(xm, wv).reshape(B, S, M_HEADS, M_HEAD_DIM)
    h = mlstm_chunkwise(q, k, v, ip + b_i, fp + b_f)
    h = head_layernorm(h, g_head).reshape(B, S, M_WIDTH).astype(xm.dtype)
    h = jax.nn.sigmoid(om) * h
    return (h + skip * xc) * jax.nn.silu(zm)


def dsa_branch(qa, ckv, za, qi, ki, wi, g_ckv, w_uk, w_uv, g_kidx, b_kidx):
    B, S, _ = qa.shape
    nb = S // Q_BLOCK
    topk = min(IDX_TOPK, S // 4)
    f32 = jnp.float32
    c = rmsnorm(ckv, g_ckv)
    kidx = layernorm(ki, g_kidx, b_kidx)
    qa = qa.reshape(B, S, A_HEADS, A_HEAD_DIM)
    qi = qi.reshape(B, S, IDX_HEADS, IDX_DIM)
    wi = wi.astype(f32) * (IDX_HEADS ** -0.5) * (IDX_DIM ** -0.5)
    s_pos = jnp.arange(S)

    def blocks(a):
        return jnp.moveaxis(a.reshape(B, nb, Q_BLOCK, *a.shape[2:]), 1, 0)

    def one_block(args):
        blk, q_b, qi_b, wi_b = args
        t_pos = blk * Q_BLOCK + jnp.arange(Q_BLOCK)
        logits = jnp.einsum('bthi,bsi->bths', qi_b.astype(f32), kidx)
        score = jnp.einsum('bths,bth->bts', jax.nn.relu(logits), wi_b)
        score = jnp.where(s_pos[None, :] <= t_pos[:, None], score, -jnp.inf)
        _, idx = lax.top_k(score, topk)
        valid = idx <= t_pos[None, :, None]
        c_sel = jax.vmap(lambda cb, ib: cb[ib])(c, idx)
        q_lat = jnp.einsum('bthd,hcd->bthc', q_b, w_uk)
        att = jnp.einsum('bthc,btkc->bthk', q_lat, c_sel).astype(f32) * (A_HEAD_DIM ** -0.5)
        att = jnp.where(valid[:, :, None, :], att, -jnp.inf)
        p = jax.nn.softmax(att, axis=-1).astype(c.dtype)
        o_lat = jnp.einsum('bthk,btkc->bthc', p, c_sel)
        return jnp.einsum('bthc,hcd->bthd', o_lat, w_uv)

    out = lax.map(one_block, (jnp.arange(nb), blocks(qa), blocks(qi), blocks(wi)))
    out = jnp.moveaxis(out, 0, 1).reshape(B, S, A_WIDTH)
    return out * jax.nn.silu(za)


def setup_inputs(seed: int = 0) -> dict:
    key = jax.random.key(seed)
    ks = jax.random.split(key, 21)
    nrm = lambda k, shape, s: jax.random.normal(k, shape, jnp.float32) * s
    nblk = M_WIDTH // M_QKV_BLOCK
    return {
        "x": nrm(ks[0], (BATCH, SEQ, D_MODEL), 1.0),
        "g_norm": 1.0 + nrm(ks[1], (DEPTH, D_MODEL), 0.02),
        "w_in": nrm(ks[2], (DEPTH, D_MODEL, N_IN), D_MODEL ** -0.5),
        "conv_w": nrm(ks[3], (DEPTH, M_CONV, M_WIDTH), M_CONV ** -0.5),
        "conv_b": nrm(ks[4], (DEPTH, M_WIDTH), 0.02),
        "w_q_m": nrm(ks[5], (DEPTH, nblk, M_QKV_BLOCK, M_QKV_BLOCK), M_QKV_BLOCK ** -0.5),
        "w_k_m": nrm(ks[6], (DEPTH, nblk, M_QKV_BLOCK, M_QKV_BLOCK), M_QKV_BLOCK ** -0.5),
        "w_v_m": nrm(ks[7], (DEPTH, nblk, M_QKV_BLOCK, M_QKV_BLOCK), M_QKV_BLOCK ** -0.5),
        "b_i": nrm(ks[8], (DEPTH, M_HEADS), 0.1),
        "b_f": jnp.linspace(3.0, 6.0, M_HEADS, dtype=jnp.float32)[None, :] + nrm(ks[9], (DEPTH, M_HEADS), 0.1),
        "g_head_m": 1.0 + nrm(ks[10], (DEPTH, M_WIDTH), 0.02),
        "skip_m": 1.0 + nrm(ks[11], (DEPTH, M_WIDTH), 0.02),
        "g_ckv": 1.0 + nrm(ks[12], (DEPTH, A_LATENT), 0.02),
        "w_uk": nrm(ks[13], (DEPTH, A_HEADS, A_LATENT, A_HEAD_DIM), A_LATENT ** -0.5),
        "w_uv": nrm(ks[14], (DEPTH, A_HEADS, A_LATENT, A_HEAD_DIM), A_LATENT ** -0.5),
        "g_kidx": 1.0 + nrm(ks[15], (DEPTH, IDX_DIM), 0.02),
        "b_kidx": nrm(ks[16], (DEPTH, IDX_DIM), 0.02),
        "w_bm": nrm(ks[17], (DEPTH, M_WIDTH, D_MODEL), M_WIDTH ** -0.5),
        "w_ba": nrm(ks[18], (DEPTH, A_WIDTH, D_MODEL), A_WIDTH ** -0.5),
        "w_out": nrm(ks[19], (DEPTH, D_MODEL, D_MODEL), D_MODEL ** -0.5),
        "g_final": 1.0 + nrm(ks[20], (D_MODEL,), 0.02),
    }


def reference(x, g_norm, w_in, conv_w, conv_b, w_q_m, w_k_m, w_v_m, b_i, b_f, g_head_m, skip_m,
              g_ckv, w_uk, w_uv, g_kidx, b_kidx, w_bm, w_ba, w_out, g_final):
    split_at = [int(s) for s in np.cumsum(SPLITS)[:-1]]
    for l in range(DEPTH):
        xn = rmsnorm(x, g_norm[l])
        proj = xn @ w_in[l]
        (xm, om, zm, ip, fp, qa, ckv, za, qi, ki, wi, gm, ga) = jnp.split(proj, split_at, axis=-1)
        y_m = mlstm_branch(xm, om, zm, ip, fp, conv_w[l], conv_b[l], w_q_m[l], w_k_m[l], w_v_m[l],
                           b_i[l], b_f[l], g_head_m[l], skip_m[l])
        y_a = dsa_branch(qa, ckv, za, qi, ki, wi, g_ckv[l], w_uk[l], w_uv[l],
                         g_kidx[l], b_kidx[l])
        merged = jax.nn.sigmoid(gm) * (y_m @ w_bm[l]) + jax.nn.sigmoid(ga) * (y_a @ w_ba[l])
        x = x + merged @ w_out[l]
    return rmsnorm(x, g_final)
```

```python
import functools

import jax
import jax.numpy as jnp
from jax import lax
from jax.experimental import pallas as pl
from jax.experimental.pallas import tpu as pltpu

EPS = 1e-6
M_QKV_BLOCK = 4
M_CONV = 4
M_CHUNK = 64
IDX_TOPK = 256
Q_BLOCK = 128
LANES = 128
VMEM_LIMIT = 56 * 1024 * 1024

F32 = jnp.float32
BF16 = jnp.bfloat16
INT_MIN = -2 ** 31


def _cparams(sem):
    return pltpu.CompilerParams(dimension_semantics=sem, vmem_limit_bytes=VMEM_LIMIT)


def _rmsnorm_kernel(x_ref, g_ref, o_ref):
    x = x_ref[...]
    y = x * lax.rsqrt(jnp.mean(x * x, axis=-1, keepdims=True) + EPS)
    o_ref[...] = (y * g_ref[...]).astype(o_ref.dtype)


def _rmsnorm(x2d, g, out_dtype, tm=256):
    T, D = x2d.shape
    tm = min(tm, T)
    return pl.pallas_call(
        _rmsnorm_kernel,
        grid=(T // tm,),
        in_specs=[pl.BlockSpec((tm, D), lambda i: (i, 0)),
                  pl.BlockSpec((1, D), lambda i: (0, 0))],
        out_specs=pl.BlockSpec((tm, D), lambda i: (i, 0)),
        out_shape=jax.ShapeDtypeStruct((T, D), out_dtype),
        compiler_params=_cparams(("parallel",)),
        name="rmsnorm",
    )(x2d, g.reshape(1, D))


def _mm_kernel(a_ref, w_ref, o_ref):
    o_ref[...] = jnp.dot(a_ref[...], w_ref[...], preferred_element_type=F32).astype(o_ref.dtype)


def _matmul(a, w, out_dtype, tm=1024, tn=512):
    M, K = a.shape
    N = w.shape[1]
    tm, tn = min(tm, M), min(tn, N)
    return pl.pallas_call(
        _mm_kernel,
        grid=(M // tm, N // tn),
        in_specs=[pl.BlockSpec((tm, K), lambda i, j: (i, 0)),
                  pl.BlockSpec((K, tn), lambda i, j: (0, j))],
        out_specs=pl.BlockSpec((tm, tn), lambda i, j: (i, j)),
        out_shape=jax.ShapeDtypeStruct((M, N), out_dtype),
        compiler_params=_cparams(("parallel", "parallel")),
        name="in_proj",
    )(a, w)


def _merge_kernel(ym_ref, ya_ref, wbm_ref, wba_ref, gm_ref, ga_ref, o_ref):
    pm = jnp.dot(ym_ref[...], wbm_ref[...], preferred_element_type=F32)
    pa = jnp.dot(ya_ref[...], wba_ref[...], preferred_element_type=F32)
    o = jax.nn.sigmoid(gm_ref[...]) * pm + jax.nn.sigmoid(ga_ref[...]) * pa
    o_ref[...] = o.astype(o_ref.dtype)


def _merge(ym, ya, wbm, wba, proj, gm_col, ga_col, tm=512, tn=512):
    M, K = ym.shape
    N = wbm.shape[1]
    tm, tn = min(tm, M), min(tn, N)
    gm_blk, ga_blk = gm_col // tn, ga_col // tn
    return pl.pallas_call(
        _merge_kernel,
        grid=(M // tm, N // tn),
        in_specs=[pl.BlockSpec((tm, K), lambda i, j: (i, 0)),
                  pl.BlockSpec((tm, K), lambda i, j: (i, 0)),
                  pl.BlockSpec((K, tn), lambda i, j: (0, j)),
                  pl.BlockSpec((K, tn), lambda i, j: (0, j)),
                  pl.BlockSpec((tm, tn), lambda i, j: (i, gm_blk + j)),
                  pl.BlockSpec((tm, tn), lambda i, j: (i, ga_blk + j))],
        out_specs=pl.BlockSpec((tm, tn), lambda i, j: (i, j)),
        out_shape=jax.ShapeDtypeStruct((M, N), BF16),
        compiler_params=_cparams(("parallel", "parallel")),
        name="merge",
    )(ym, ya, wbm, wba, proj, proj)


def _out_kernel(a_ref, w_ref, x_ref, o_ref):
    o_ref[...] = x_ref[...] + jnp.dot(a_ref[...], w_ref[...], preferred_element_type=F32)


def _out_proj(a, w, x2d, tm=1024, tn=512):
    M, K = a.shape
    N = w.shape[1]
    tm, tn = min(tm, M), min(tn, N)
    return pl.pallas_call(
        _out_kernel,
        grid=(M // tm, N // tn),
        in_specs=[pl.BlockSpec((tm, K), lambda i, j: (i, 0)),
                  pl.BlockSpec((K, tn), lambda i, j: (0, j)),
                  pl.BlockSpec((tm, tn), lambda i, j: (i, j))],
        out_specs=pl.BlockSpec((tm, tn), lambda i, j: (i, j)),
        out_shape=jax.ShapeDtypeStruct((M, N), F32),
        compiler_params=_cparams(("parallel", "parallel")),
        name="out_proj",
    )(a, w, x2d)


def _log_sigmoid(x):
    return -(jnp.maximum(-x, 0.0) + jnp.log1p(jnp.exp(-jnp.abs(x))))


def _mlstm_kernel(xm_ref, om_ref, zm_ref, sm_ref, cw_ref, cb_ref, wq_ref, wk_ref, wv_ref,
                  gb_ref, gh_ref, sk_ref, y_ref,
                  xbuf, xc_s, q_s, k_s, v_s, c_s, n_s, m_s, *, n_heads, rows):
    h = pl.program_id(1)
    j = pl.program_id(2)
    dh = xc_s.shape[1]
    L = M_CHUNK
    pad = 8

    @pl.when(j == 0)
    def _():
        xbuf[0:pad, :] = jnp.zeros((pad, dh), F32)
        c_s[...] = jnp.zeros_like(c_s)
        n_s[...] = jnp.zeros_like(n_s)
        m_s[...] = jnp.zeros_like(m_s)

    xblk = xm_ref[...]
    xbuf[pad:pad + rows, :] = xblk
    acc = jnp.zeros((rows, dh), F32) + cb_ref[...]
    for k in range(M_CONV):
        acc = acc + cw_ref[k:k + 1, :] * xbuf[pl.ds(pad - (M_CONV - 1) + k, rows), :]
    xc = acc * jax.nn.sigmoid(acc)
    xc_s[...] = xc
    xbuf[0:pad, :] = xblk[rows - pad:rows, :]

    scale = dh ** -0.5
    for t in range(dh // LANES):
        sl = slice(t * LANES, (t + 1) * LANES)
        xct = xc[:, sl].astype(BF16)
        xmt = xblk[:, sl].astype(BF16)
        q_s[:, sl] = jnp.dot(xct, wq_ref[t], preferred_element_type=F32).astype(BF16)
        k_s[:, sl] = jnp.dot(xct, wk_ref[t], preferred_element_type=F32) * scale
        v_s[:, sl] = jnp.dot(xmt, wv_ref[t], preferred_element_type=F32).astype(BF16)

    lane = lax.broadcasted_iota(jnp.int32, (L, LANES), 1)
    row_i = lax.broadcasted_iota(jnp.int32, (L, L), 0)
    col_i = lax.broadcasted_iota(jnp.int32, (L, L), 1)
    tril = col_i <= row_i
    eye = col_i == row_i

    def to_row(col):
        return jnp.sum(jnp.where(eye, col, 0.0), axis=0, keepdims=True)

    def chunk(c, carry):
        r0 = pl.multiple_of(c * L, L)
        rs = pl.ds(r0, L)
        g = sm_ref[rs, :] + gb_ref[...]
        i_col = jnp.sum(jnp.where(lane == h, g, 0.0), axis=1, keepdims=True)
        f_col = _log_sigmoid(jnp.sum(jnp.where(lane == n_heads + h, g, 0.0), axis=1, keepdims=True))
        i_row = to_row(i_col)
        f_row = to_row(f_col)
        b_col = jnp.sum(jnp.where(tril, f_row, 0.0), axis=1, keepdims=True)
        b_row = jnp.sum(jnp.where(col_i >= row_i, f_col, 0.0), axis=0, keepdims=True)
        m_prev = m_s[...]
        dmat = jnp.where(tril, b_col - b_row + i_row, -jnp.inf)
        m_inter = b_col + m_prev
        m_t = jnp.maximum(m_inter, jnp.max(dmat, axis=1, keepdims=True))
        qc = q_s[rs, :]
        kc = k_s[rs, :]
        vc = v_s[rs, :]
        s_qk = lax.dot_general(qc, kc.astype(BF16), (((1,), (1,)), ((), ())),
                               preferred_element_type=F32)
        a = s_qk * jnp.exp(dmat - m_t)
        inter = jnp.exp(m_inter - m_t)
        c_prev = c_s[...]
        num = (jnp.dot(a.astype(BF16), vc, preferred_element_type=F32)
               + inter * jnp.dot(qc, c_prev.astype(BF16), preferred_element_type=F32))
        qn = jnp.sum(qc.astype(F32) * n_s[...], axis=1, keepdims=True)
        den = jnp.sum(a, axis=1, keepdims=True) + inter * qn
        hh = num / jnp.maximum(jnp.abs(den), jnp.exp(-m_t))

        mu = jnp.mean(hh, axis=1, keepdims=True)
        hc = hh - mu
        hn = hc * lax.rsqrt(jnp.mean(hc * hc, axis=1, keepdims=True) + EPS) * gh_ref[...]
        hn = jax.nn.sigmoid(om_ref[rs, :]) * hn
        zm = zm_ref[rs, :]
        y = (hn + sk_ref[...] * xc_s[rs, :]) * (zm * jax.nn.sigmoid(zm))
        y_ref[rs, :] = y.astype(y_ref.dtype)

        b_last = b_col[L - 1:L, :]
        g_col = b_last - b_col + i_col
        m_new = jnp.maximum(b_last + m_prev, jnp.max(g_col, axis=0, keepdims=True))
        w_col = jnp.exp(g_col - m_new)
        decay = jnp.exp(b_last + m_prev - m_new)
        kw = kc * w_col
        c_s[...] = decay * c_prev + lax.dot_general(
            kw.astype(BF16), vc, (((0,), (0,)), ((), ())), preferred_element_type=F32)
        n_s[...] = decay * n_s[...] + jnp.sum(kw, axis=0, keepdims=True)
        m_s[...] = m_new
        return carry

    lax.fori_loop(0, rows // L, chunk, 0)


def _mlstm(proj3, col_xm, col_om, col_zm, col_small, conv_w, conv_b, wq_d, wk_d, wv_d,
           gate_bias, g_head, skip, n_heads, rows=256):
    B, S, _ = proj3.shape
    dm = conv_w.shape[1]
    dh = dm // n_heads
    rows = min(rows, S)
    nt = dh // LANES
    bx, bo, bz, bs = col_xm // dh, col_om // dh, col_zm // dh, col_small // LANES
    seq = lambda base: pl.BlockSpec((None, rows, dh), lambda b, h, j: (b, j, base + h))
    vec = pl.BlockSpec((1, dh), lambda b, h, j: (0, h))
    wsp = pl.BlockSpec((nt, LANES, LANES), lambda b, h, j: (h, 0, 0))
    kern = functools.partial(_mlstm_kernel, n_heads=n_heads, rows=rows)
    return pl.pallas_call(
        kern,
        grid=(B, n_heads, S // rows),
        in_specs=[seq(bx), seq(bo), seq(bz),
                  pl.BlockSpec((None, rows, LANES), lambda b, h, j: (b, j, bs)),
                  pl.BlockSpec((M_CONV, dh), lambda b, h, j: (0, h)),
                  vec, wsp, wsp, wsp,
                  pl.BlockSpec((1, LANES), lambda b, h, j: (0, 0)),
                  vec, vec],
        out_specs=pl.BlockSpec((None, rows, dh), lambda b, h, j: (b, j, h)),
        out_shape=jax.ShapeDtypeStruct((B, S, dm), BF16),
        scratch_shapes=[pltpu.VMEM((rows + 8, dh), F32),
                        pltpu.VMEM((rows, dh), F32),
                        pltpu.VMEM((rows, dh), BF16),
                        pltpu.VMEM((rows, dh), F32),
                        pltpu.VMEM((rows, dh), BF16),
                        pltpu.VMEM((dh, dh), F32),
                        pltpu.VMEM((1, dh), F32),
                        pltpu.VMEM((1, 1), F32)],
        compiler_params=_cparams(("parallel", "parallel", "arbitrary")),
        name="mlstm",
    )(proj3, proj3, proj3, proj3, conv_w, conv_b.reshape(1, dm), wq_d, wk_d, wv_d,
      gate_bias, g_head.reshape(1, dm), skip.reshape(1, dm))


def _dsa_prep_kernel(ckv_ref, ki_ref, gc_ref, gk_ref, bk_ref, c_ref, kidx_ref):
    x = ckv_ref[...]
    c = x * lax.rsqrt(jnp.mean(x * x, axis=-1, keepdims=True) + EPS) * gc_ref[...]
    c_ref[...] = c.astype(c_ref.dtype)
    k = ki_ref[...]
    mu = jnp.mean(k, axis=-1, keepdims=True)
    kc = k - mu
    y = kc * lax.rsqrt(jnp.mean(kc * kc, axis=-1, keepdims=True) + EPS)
    kidx_ref[...] = (y * gk_ref[...] + bk_ref[...]).astype(kidx_ref.dtype)


def _dsa_prep(proj, col_ckv, col_ki, g_ckv, g_kidx, b_kidx, tm=512):
    T = proj.shape[0]
    dc, di = g_ckv.shape[0], g_kidx.shape[0]
    tm = min(tm, T)
    return pl.pallas_call(
        _dsa_prep_kernel,
        grid=(T // tm,),
        in_specs=[pl.BlockSpec((tm, dc), lambda i: (i, col_ckv // dc)),
                  pl.BlockSpec((tm, di), lambda i: (i, col_ki // di)),
                  pl.BlockSpec((1, dc), lambda i: (0, 0)),
                  pl.BlockSpec((1, di), lambda i: (0, 0)),
                  pl.BlockSpec((1, di), lambda i: (0, 0))],
        out_specs=[pl.BlockSpec((tm, dc), lambda i: (i, 0)),
                   pl.BlockSpec((tm, di), lambda i: (i, 0))],
        out_shape=[jax.ShapeDtypeStruct((T, dc), BF16),
                   jax.ShapeDtypeStruct((T, di), BF16)],
        compiler_params=_cparams(("parallel",)),
        name="dsa_prep",
    )(proj, proj, g_ckv.reshape(1, dc), g_kidx.reshape(1, di), b_kidx.reshape(1, di))


def _indexer_kernel(qi_ref, sm_ref, kidx_ref, mask_ref, key_s, *, n_idx, wi_lane, topk):
    qb = pl.program_id(1)
    S = kidx_ref.shape[0]
    di = kidx_ref.shape[1]
    wscale = (n_idx ** -0.5) * (di ** -0.5)
    kidx = kidx_ref[...]
    sm = sm_ref[...]
    lane = lax.broadcasted_iota(jnp.int32, (Q_BLOCK, LANES), 1)

    def head(hh, score):
        c0 = pl.multiple_of(hh * di, di)
        qh = qi_ref[:, pl.ds(c0, di)].astype(BF16)
        logits = lax.dot_general(qh, kidx, (((1,), (1,)), ((), ())), preferred_element_type=F32)
        w = jnp.sum(jnp.where(lane == wi_lane + hh, sm, 0.0), axis=1, keepdims=True) * wscale
        return score + jnp.maximum(logits, 0.0) * w

    score = lax.fori_loop(0, n_idx, head, jnp.zeros((Q_BLOCK, S), F32))

    bits = pltpu.bitcast(score, jnp.int32)
    key = jnp.where(bits >= 0, bits, bits ^ jnp.int32(0x7FFFFFFF))
    t_pos = qb * Q_BLOCK + lax.broadcasted_iota(jnp.int32, (Q_BLOCK, S), 0)
    s_pos = lax.broadcasted_iota(jnp.int32, (Q_BLOCK, S), 1)
    causal = s_pos <= t_pos
    key_s[...] = jnp.where(causal, key, jnp.int32(INT_MIN))

    def count_ge(thr):
        return jnp.sum(jnp.where(key_s[...] >= thr, 1.0, 0.0), axis=1, keepdims=True)

    kf = float(topk)
    thr = jnp.where(count_ge(jnp.zeros((Q_BLOCK, 1), jnp.int32)) >= kf,
                    jnp.int32(0), jnp.int32(INT_MIN))

    def bit_step(it, thr):
        cand = thr + jnp.left_shift(jnp.int32(1), 30 - it)
        return jnp.where(count_ge(cand) >= kf, cand, thr)

    thr = lax.fori_loop(0, 31, bit_step, thr)
    sel = jnp.logical_and(key_s[...] >= thr, causal)
    mask_ref[...] = jnp.where(sel, 1, 0).astype(mask_ref.dtype)


def _indexer(proj3, col_qi, col_small, kidx3, n_idx, wi_lane, topk):
    B, S, di = kidx3.shape
    wq = n_idx * di
    kern = functools.partial(_indexer_kernel, n_idx=n_idx, wi_lane=wi_lane, topk=topk)
    return pl.pallas_call(
        kern,
        grid=(B, S // Q_BLOCK),
        in_specs=[pl.BlockSpec((None, Q_BLOCK, wq), lambda b, q: (b, q, col_qi // wq)),
                  pl.BlockSpec((None, Q_BLOCK, LANES), lambda b, q: (b, q, col_small // LANES)),
                  pl.BlockSpec((None, S, di), lambda b, q: (b, 0, 0))],
        out_specs=pl.BlockSpec((None, Q_BLOCK, S), lambda b, q: (b, q, 0)),
        out_shape=jax.ShapeDtypeStruct((B, S, S), jnp.int8),
        scratch_shapes=[pltpu.VMEM((Q_BLOCK, S), jnp.int32)],
        compiler_params=_cparams(("parallel", "parallel")),
        name="indexer",
    )(proj3, proj3, kidx3)


def _attn_kernel(qa_ref, za_ref, mask_ref, c_ref, wuk_ref, wuv_ref, y_ref, *, heads):
    dh = wuk_ref.shape[2]
    scale = dh ** -0.5
    cc = c_ref[...]
    keep = mask_ref[...].astype(jnp.int32) != 0
    for g in range(heads):
        sl = slice(g * dh, (g + 1) * dh)
        qh = qa_ref[:, sl].astype(BF16)
        qlat = lax.dot_general(qh, wuk_ref[g], (((1,), (1,)), ((), ())), preferred_element_type=F32)
        att = lax.dot_general(qlat.astype(BF16), cc, (((1,), (1,)), ((), ())),
                              preferred_element_type=F32) * scale
        att = jnp.where(keep, att, -jnp.inf)
        mx = jnp.max(att, axis=1, keepdims=True)
        p = jnp.exp(att - mx)
        p = p / jnp.sum(p, axis=1, keepdims=True)
        olat = jnp.dot(p.astype(BF16), cc, preferred_element_type=F32)
        o = jnp.dot(olat.astype(BF16), wuv_ref[g], preferred_element_type=F32)
        za = za_ref[:, sl]
        y_ref[:, sl] = (o * (za * jax.nn.sigmoid(za))).astype(y_ref.dtype)


def _attention(proj3, col_qa, col_za, mask, c3, wuk, wuv, heads_per_step=8):
    B, S, dc = c3.shape
    n_heads, _, dh = wuk.shape
    G = min(heads_per_step, n_heads)
    gw = G * dh
    kern = functools.partial(_attn_kernel, heads=G)
    return pl.pallas_call(
        kern,
        grid=(B, S // Q_BLOCK, n_heads // G),
        in_specs=[pl.BlockSpec((None, Q_BLOCK, gw), lambda b, q, g: (b, q, col_qa // gw + g)),
                  pl.BlockSpec((None, Q_BLOCK, gw), lambda b, q, g: (b, q, col_za // gw + g)),
                  pl.BlockSpec((None, Q_BLOCK, S), lambda b, q, g: (b, q, 0)),
                  pl.BlockSpec((None, S, dc), lambda b, q, g: (b, 0, 0)),
                  pl.BlockSpec((G, dc, dh), lambda b, q, g: (g, 0, 0)),
                  pl.BlockSpec((G, dc, dh), lambda b, q, g: (g, 0, 0))],
        out_specs=pl.BlockSpec((None, Q_BLOCK, gw), lambda b, q, g: (b, q, g)),
        out_shape=jax.ShapeDtypeStruct((B, S, n_heads * dh), BF16),
        compiler_params=_cparams(("parallel", "parallel", "parallel")),
        name="attention",
    )(proj3, proj3, mask, c3, wuk, wuv)


def _blockdiag_dense(w):
    nblk, blk, _ = w.shape
    per = LANES // blk
    w4 = w.reshape(nblk // per, per, blk, blk)
    eye = jnp.eye(per, dtype=w.dtype)
    dense = jnp.einsum('tgio,gh->tgiho', w4, eye)
    return dense.reshape(nblk // per, LANES, LANES).astype(BF16)


def kernel(x, g_norm, w_in, conv_w, conv_b, w_q_m, w_k_m, w_v_m, b_i, b_f, g_head_m, skip_m, g_ckv, w_uk, w_uv, g_kidx, b_kidx, w_bm, w_ba, w_out, g_final):
    B, S, D = x.shape
    depth = g_norm.shape[0]
    T = B * S
    dm = conv_w.shape[-1]
    mh = b_i.shape[-1]
    ah, dc, adh = w_uk.shape[1:]
    da = ah * adh
    di = g_kidx.shape[-1]
    n_in = w_in.shape[-1]
    n_idx = (n_in - (3 * dm + 2 * mh + da + dc + da + di + 2 * D)) // (di + 1)
    topk = min(IDX_TOPK, S // 4)
    assert 2 * mh + n_idx <= LANES

    sizes = (dm, dm, dm, mh, mh, da, dc, da, n_idx * di, di, n_idx, D, D)
    offs = [0]
    for s in sizes:
        offs.append(offs[-1] + s)
    (o_xm, o_om, o_zm, o_ip, o_fp, o_qa, o_ckv, o_za, o_qi, o_ki, o_wi, o_gm, o_ga) = offs[:-1]

    order = [(o_xm, dm), (o_om, dm), (o_zm, dm), (o_qa, da), (o_za, da), (o_qi, n_idx * di),
             (o_gm, D), (o_ga, D), (o_ckv, dc), (o_ki, di)]
    cols, pos = {}, 0
    for off, width in order:
        cols[off] = pos
        pos += width
    c_small = pos
    pos += LANES
    tn = 512
    n_pad = -(-pos // tn) * tn

    x2 = x.reshape(T, D)
    for l in range(depth):
        w = w_in[l]
        parts = [w[:, off:off + width] for off, width in order]
        parts += [w[:, o_ip:o_ip + mh], w[:, o_fp:o_fp + mh], w[:, o_wi:o_wi + n_idx],
                  jnp.zeros((D, n_pad - c_small - 2 * mh - n_idx), w.dtype)]
        w_cat = jnp.concatenate(parts, axis=1).astype(BF16)

        xn = _rmsnorm(x2, g_norm[l], BF16)
        proj = _matmul(xn, w_cat, F32, tn=tn)
        proj3 = proj.reshape(B, S, n_pad)

        gate_bias = jnp.concatenate([b_i[l], b_f[l], jnp.zeros((LANES - 2 * mh,), F32)]).reshape(1, LANES)
        y_m = _mlstm(proj3, cols[o_xm], cols[o_om], cols[o_zm], c_small, conv_w[l], conv_b[l],
                     _blockdiag_dense(w_q_m[l]), _blockdiag_dense(w_k_m[l]), _blockdiag_dense(w_v_m[l]),
                     gate_bias, g_head_m[l], skip_m[l], mh)

        c2, kidx2 = _dsa_prep(proj, cols[o_ckv], cols[o_ki], g_ckv[l], g_kidx[l], b_kidx[l])
        mask = _indexer(proj3, cols[o_qi], c_small, kidx2.reshape(B, S, di), n_idx, 2 * mh, topk)
        y_a = _attention(proj3, cols[o_qa], cols[o_za], mask, c2.reshape(B, S, dc),
                         w_uk[l].astype(BF16), w_uv[l].astype(BF16))

        merged = _merge(y_m.reshape(T, dm), y_a.reshape(T, da), w_bm[l].astype(BF16),
                        w_ba[l].astype(BF16), proj, cols[o_gm], cols[o_ga])
        x2 = _out_proj(merged, w_out[l].astype(BF16), x2)
    out = _rmsnorm(x2, g_final, F32)
    return out.reshape(B, S, D)
```

```python
import functools
import math

import jax
import jax.numpy as jnp
from jax import lax
from jax.experimental import pallas as pl
from jax.experimental.pallas import tpu as pltpu

EPS = 1e-6
M_QKV_BLOCK = 4
M_CONV = 4
M_CHUNK = 64
IDX_TOPK = 256
Q_BLOCK = 128
Q_GROUP = 256
LANES = 128
SUBLANES = 8
VMEM_LIMIT = 56 * 1024 * 1024

F32 = jnp.float32
BF16 = jnp.bfloat16
INT_MIN = -2 ** 31


def _cparams(sem):
    return pltpu.CompilerParams(dimension_semantics=sem, vmem_limit_bytes=VMEM_LIMIT)


def _rmsnorm_kernel(x_ref, g_ref, o_ref):
    x = x_ref[...]
    y = x * lax.rsqrt(jnp.mean(x * x, axis=-1, keepdims=True) + EPS)
    o_ref[...] = (y * g_ref[...]).astype(o_ref.dtype)


def _rmsnorm(x2d, g, out_dtype, tm=256):
    T, D = x2d.shape
    tm = min(tm, T)
    return pl.pallas_call(
        _rmsnorm_kernel,
        grid=(T // tm,),
        in_specs=[pl.BlockSpec((tm, D), lambda i: (i, 0)),
                  pl.BlockSpec((1, D), lambda i: (0, 0))],
        out_specs=pl.BlockSpec((tm, D), lambda i: (i, 0)),
        out_shape=jax.ShapeDtypeStruct((T, D), out_dtype),
        compiler_params=_cparams(("parallel",)),
        name="rmsnorm",
    )(x2d, g.reshape(1, D))


def _mm_kernel(a_ref, w_ref, o_ref):
    o_ref[...] = jnp.dot(a_ref[...], w_ref[...], preferred_element_type=F32).astype(o_ref.dtype)


def _matmul(a, w, out_dtype, tm=1024, tn=512):
    M, K = a.shape
    N = w.shape[1]
    tm, tn = min(tm, M), min(tn, N)
    return pl.pallas_call(
        _mm_kernel,
        grid=(M // tm, N // tn),
        in_specs=[pl.BlockSpec((tm, K), lambda i, j: (i, 0)),
                  pl.BlockSpec((K, tn), lambda i, j: (0, j))],
        out_specs=pl.BlockSpec((tm, tn), lambda i, j: (i, j)),
        out_shape=jax.ShapeDtypeStruct((M, N), out_dtype),
        compiler_params=_cparams(("parallel", "parallel")),
        name="in_proj",
    )(a, w)


def _merge_kernel(ym_ref, ya_ref, wbm_ref, wba_ref, gm_ref, ga_ref, o_ref):
    pm = jnp.dot(ym_ref[...], wbm_ref[...], preferred_element_type=F32)
    pa = jnp.dot(ya_ref[...], wba_ref[...], preferred_element_type=F32)
    o = jax.nn.sigmoid(gm_ref[...]) * pm + jax.nn.sigmoid(ga_ref[...]) * pa
    o_ref[...] = o.astype(o_ref.dtype)


def _merge(ym, ya, wbm, wba, proj, gm_col, ga_col, tm=512, tn=512):
    M, K = ym.shape
    N = wbm.shape[1]
    tm, tn = min(tm, M), min(tn, N)
    gm_blk, ga_blk = gm_col // tn, ga_col // tn
    return pl.pallas_call(
        _merge_kernel,
        grid=(M // tm, N // tn),
        in_specs=[pl.BlockSpec((tm, K), lambda i, j: (i, 0)),
                  pl.BlockSpec((tm, K), lambda i, j: (i, 0)),
                  pl.BlockSpec((K, tn), lambda i, j: (0, j)),
                  pl.BlockSpec((K, tn), lambda i, j: (0, j)),
                  pl.BlockSpec((tm, tn), lambda i, j: (i, gm_blk + j)),
                  pl.BlockSpec((tm, tn), lambda i, j: (i, ga_blk + j))],
        out_specs=pl.BlockSpec((tm, tn), lambda i, j: (i, j)),
        out_shape=jax.ShapeDtypeStruct((M, N), BF16),
        compiler_params=_cparams(("parallel", "parallel")),
        name="merge",
    )(ym, ya, wbm, wba, proj, proj)


def _out_kernel(a_ref, w_ref, x_ref, o_ref):
    o_ref[...] = x_ref[...] + jnp.dot(a_ref[...], w_ref[...], preferred_element_type=F32)


def _out_proj(a, w, x2d, tm=1024, tn=512):
    M, K = a.shape
    N = w.shape[1]
    tm, tn = min(tm, M), min(tn, N)
    return pl.pallas_call(
        _out_kernel,
        grid=(M // tm, N // tn),
        in_specs=[pl.BlockSpec((tm, K), lambda i, j: (i, 0)),
                  pl.BlockSpec((K, tn), lambda i, j: (0, j)),
                  pl.BlockSpec((tm, tn), lambda i, j: (i, j))],
        out_specs=pl.BlockSpec((tm, tn), lambda i, j: (i, j)),
        out_shape=jax.ShapeDtypeStruct((M, N), F32),
        compiler_params=_cparams(("parallel", "parallel")),
        name="out_proj",
    )(a, w, x2d)


def _log_sigmoid(x):
    return -(jnp.maximum(-x, 0.0) + jnp.log1p(jnp.exp(-jnp.abs(x))))


def _mlstm_kernel(xm_ref, om_ref, zm_ref, sm_ref, cw_ref, cb_ref, wq_ref, wk_ref, wv_ref,
                  gb_ref, gh_ref, sk_ref, y_ref,
                  xbuf, xc_s, q_s, k_s, v_s, c_s, n_s, m_s, *, n_heads, rows):
    h = pl.program_id(1)
    j = pl.program_id(2)
    dh = xc_s.shape[1]
    L = M_CHUNK
    pad = SUBLANES

    @pl.when(j == 0)
    def _():
        xbuf[0:pad, :] = jnp.zeros((pad, dh), F32)
        c_s[...] = jnp.zeros_like(c_s)
        n_s[...] = jnp.zeros_like(n_s)
        m_s[...] = jnp.zeros_like(m_s)

    xblk = xm_ref[...]
    xbuf[pad:pad + rows, :] = xblk
    acc = jnp.zeros((rows, dh), F32) + cb_ref[...]
    for k in range(M_CONV):
        acc = acc + cw_ref[k:k + 1, :] * xbuf[pl.ds(pad - (M_CONV - 1) + k, rows), :]
    xc = acc * jax.nn.sigmoid(acc)
    xc_s[...] = xc
    xbuf[0:pad, :] = xblk[rows - pad:rows, :]

    scale = dh ** -0.5
    for t in range(dh // LANES):
        sl = slice(t * LANES, (t + 1) * LANES)
        xct = xc[:, sl].astype(BF16)
        xmt = xblk[:, sl].astype(BF16)
        q_s[:, sl] = jnp.dot(xct, wq_ref[t], preferred_element_type=F32).astype(BF16)
        k_s[:, sl] = jnp.dot(xct, wk_ref[t], preferred_element_type=F32) * scale
        v_s[:, sl] = jnp.dot(xmt, wv_ref[t], preferred_element_type=F32).astype(BF16)

    lane = lax.broadcasted_iota(jnp.int32, (L, LANES), 1)
    row_i = lax.broadcasted_iota(jnp.int32, (L, L), 0)
    col_i = lax.broadcasted_iota(jnp.int32, (L, L), 1)
    tril = col_i <= row_i
    eye = col_i == row_i

    def to_row(col):
        return jnp.sum(jnp.where(eye, col, 0.0), axis=0, keepdims=True)

    def chunk(c, carry):
        r0 = pl.multiple_of(c * L, L)
        rs = pl.ds(r0, L)
        g = sm_ref[rs, :] + gb_ref[...]
        i_col = jnp.sum(jnp.where(lane == h, g, 0.0), axis=1, keepdims=True)
        f_col = _log_sigmoid(jnp.sum(jnp.where(lane == n_heads + h, g, 0.0), axis=1, keepdims=True))
        i_row = to_row(i_col)
        f_row = to_row(f_col)
        b_col = jnp.sum(jnp.where(tril, f_row, 0.0), axis=1, keepdims=True)
        b_row = jnp.sum(jnp.where(col_i >= row_i, f_col, 0.0), axis=0, keepdims=True)
        m_prev = m_s[...]
        dmat = jnp.where(tril, b_col - b_row + i_row, -jnp.inf)
        m_inter = b_col + m_prev
        m_t = jnp.maximum(m_inter, jnp.max(dmat, axis=1, keepdims=True))
        qc = q_s[rs, :]
        kc = k_s[rs, :]
        vc = v_s[rs, :]
        s_qk = lax.dot_general(qc, kc.astype(BF16), (((1,), (1,)), ((), ())),
                               preferred_element_type=F32)
        a = s_qk * jnp.exp(dmat - m_t)
        inter = jnp.exp(m_inter - m_t)
        c_prev = c_s[...]
        num = (jnp.dot(a.astype(BF16), vc, preferred_element_type=F32)
               + inter * jnp.dot(qc, c_prev.astype(BF16), preferred_element_type=F32))
        qn = jnp.sum(qc.astype(F32) * n_s[...], axis=1, keepdims=True)
        den = jnp.sum(a, axis=1, keepdims=True) + inter * qn
        hh = num / jnp.maximum(jnp.abs(den), jnp.exp(-m_t))

        mu = jnp.mean(hh, axis=1, keepdims=True)
        hc = hh - mu
        hn = hc * lax.rsqrt(jnp.mean(hc * hc, axis=1, keepdims=True) + EPS) * gh_ref[...]
        hn = jax.nn.sigmoid(om_ref[rs, :]) * hn
        zm = zm_ref[rs, :]
        y = (hn + sk_ref[...] * xc_s[rs, :]) * (zm * jax.nn.sigmoid(zm))
        y_ref[rs, :] = y.astype(y_ref.dtype)

        b_last = b_col[L - 1:L, :]
        g_col = b_last - b_col + i_col
        m_new = jnp.maximum(b_last + m_prev, jnp.max(g_col, axis=0, keepdims=True))
        w_col = jnp.exp(g_col - m_new)
        decay = jnp.exp(b_last + m_prev - m_new)
        kw = kc * w_col
        c_s[...] = decay * c_prev + lax.dot_general(
            kw.astype(BF16), vc, (((0,), (0,)), ((), ())), preferred_element_type=F32)
        n_s[...] = decay * n_s[...] + jnp.sum(kw, axis=0, keepdims=True)
        m_s[...] = m_new
        return carry

    lax.fori_loop(0, rows // L, chunk, 0)


def _mlstm(proj3, col_xm, col_om, col_zm, col_small, conv_w, conv_b, wq_d, wk_d, wv_d,
           gate_bias, g_head, skip, n_heads, rows=256):
    B, S, _ = proj3.shape
    dm = conv_w.shape[1]
    dh = dm // n_heads
    rows = min(rows, S)
    nt = dh // LANES
    bx, bo, bz, bs = col_xm // dh, col_om // dh, col_zm // dh, col_small // LANES
    seq = lambda base: pl.BlockSpec((None, rows, dh), lambda b, h, j: (b, j, base + h))
    vec = pl.BlockSpec((1, dh), lambda b, h, j: (0, h))
    wsp = pl.BlockSpec((nt, LANES, LANES), lambda b, h, j: (h, 0, 0))
    kern = functools.partial(_mlstm_kernel, n_heads=n_heads, rows=rows)
    return pl.pallas_call(
        kern,
        grid=(B, n_heads, S // rows),
        in_specs=[seq(bx), seq(bo), seq(bz),
                  pl.BlockSpec((None, rows, LANES), lambda b, h, j: (b, j, bs)),
                  pl.BlockSpec((M_CONV, dh), lambda b, h, j: (0, h)),
                  vec, wsp, wsp, wsp,
                  pl.BlockSpec((1, LANES), lambda b, h, j: (0, 0)),
                  vec, vec],
        out_specs=pl.BlockSpec((None, rows, dh), lambda b, h, j: (b, j, h)),
        out_shape=jax.ShapeDtypeStruct((B, S, dm), BF16),
        scratch_shapes=[pltpu.VMEM((rows + SUBLANES, dh), F32),
                        pltpu.VMEM((rows, dh), F32),
                        pltpu.VMEM((rows, dh), BF16),
                        pltpu.VMEM((rows, dh), F32),
                        pltpu.VMEM((rows, dh), BF16),
                        pltpu.VMEM((dh, dh), F32),
                        pltpu.VMEM((1, dh), F32),
                        pltpu.VMEM((1, 1), F32)],
        compiler_params=_cparams(("parallel", "parallel", "arbitrary")),
        name="mlstm",
    )(proj3, proj3, proj3, proj3, conv_w, conv_b.reshape(1, dm), wq_d, wk_d, wv_d,
      gate_bias, g_head.reshape(1, dm), skip.reshape(1, dm))


def _dsa_prep_kernel(ckv_ref, ki_ref, gc_ref, gk_ref, bk_ref, c_ref, ct_ref, kidx_ref):
    x = ckv_ref[...]
    c = x * lax.rsqrt(jnp.mean(x * x, axis=-1, keepdims=True) + EPS) * gc_ref[...]
    c_ref[...] = c.astype(c_ref.dtype)
    ct_ref[...] = c.T.astype(ct_ref.dtype)
    k = ki_ref[...]
    mu = jnp.mean(k, axis=-1, keepdims=True)
    kc = k - mu
    y = kc * lax.rsqrt(jnp.mean(kc * kc, axis=-1, keepdims=True) + EPS)
    kidx_ref[...] = (y * gk_ref[...] + bk_ref[...]).astype(kidx_ref.dtype)


def _dsa_prep(proj3, col_ckv, col_ki, g_ckv, g_kidx, b_kidx, tm=512):
    B, S, _ = proj3.shape
    dc, di = g_ckv.shape[0], g_kidx.shape[0]
    tm = min(tm, S)
    return pl.pallas_call(
        _dsa_prep_kernel,
        grid=(B, S // tm),
        in_specs=[pl.BlockSpec((None, tm, dc), lambda b, i: (b, i, col_ckv // dc)),
                  pl.BlockSpec((None, tm, di), lambda b, i: (b, i, col_ki // di)),
                  pl.BlockSpec((1, dc), lambda b, i: (0, 0)),
                  pl.BlockSpec((1, di), lambda b, i: (0, 0)),
                  pl.BlockSpec((1, di), lambda b, i: (0, 0))],
        out_specs=[pl.BlockSpec((None, tm, dc), lambda b, i: (b, i, 0)),
                   pl.BlockSpec((None, dc, tm), lambda b, i: (b, 0, i)),
                   pl.BlockSpec((None, tm, di), lambda b, i: (b, i, 0))],
        out_shape=[jax.ShapeDtypeStruct((B, S, dc), BF16),
                   jax.ShapeDtypeStruct((B, dc, S), BF16),
                   jax.ShapeDtypeStruct((B, S, di), BF16)],
        compiler_params=_cparams(("parallel", "parallel")),
        name="dsa_prep",
    )(proj3, proj3, g_ckv.reshape(1, dc), g_kidx.reshape(1, di), b_kidx.reshape(1, di))


def _indexer_kernel(qi_ref, sm_ref, kidx_ref, mask_ref, score_s, key_s, wt_s,
                    *, n_idx, wi_lane, topk, q0):
    lk, di = kidx_ref.shape
    qg = qi_ref.shape[0]
    kf = float(topk)
    wt_s[...] = sm_ref[...].T * ((n_idx ** -0.5) * (di ** -0.5))
    kidx = kidx_ref[...]
    score_s[...] = jnp.zeros_like(score_s)

    def head(hh, carry):
        c0 = pl.multiple_of(hh * di, di)
        qh = qi_ref[:, pl.ds(c0, di)].astype(BF16)
        logits = lax.dot_general(kidx, qh, (((1,), (1,)), ((), ())), preferred_element_type=F32)
        w = wt_s[pl.ds(wi_lane + hh, 1), :]
        score_s[...] += jnp.maximum(logits, 0.0) * w
        return carry

    lax.fori_loop(0, n_idx, head, 0)

    bits = pltpu.bitcast(score_s[...], jnp.int32)
    key = jnp.where(bits >= 0, bits, bits ^ jnp.int32(0x7FFFFFFF))
    s_pos = lax.broadcasted_iota(jnp.int32, (lk, qg), 0)
    t_pos = q0 + lax.broadcasted_iota(jnp.int32, (lk, qg), 1)
    causal = s_pos <= t_pos
    key_s[...] = jnp.where(causal, key, jnp.int32(INT_MIN))

    def count(pred):
        return jnp.sum(jnp.where(pred, 1.0, 0.0), axis=0, keepdims=True)

    thr = jnp.where(count(key_s[...] >= 0) >= kf, jnp.int32(0), jnp.int32(INT_MIN))

    def bit_step(it, thr):
        cand = thr + jnp.left_shift(jnp.int32(1), 30 - it)
        return jnp.where(count(key_s[...] >= cand) >= kf, cand, thr)

    thr = lax.fori_loop(0, 31, bit_step, thr)
    key = key_s[...]
    ge = key >= thr
    mask_ref[...] = jnp.where(jnp.logical_and(ge, causal), 1, 0).astype(mask_ref.dtype)

    excess = jnp.logical_and(count(ge) > kf, thr > jnp.int32(INT_MIN))

    @pl.when(jnp.max(jnp.where(excess, 1.0, 0.0)) > 0.0)
    def _():
        gt = key > thr
        tie = key == thr
        need = kf - count(gt)
        nbits = max(1, (lk - 1).bit_length())

        def idx_step(it, x):
            cand = x + jnp.left_shift(jnp.int32(1), nbits - 1 - it)
            c = count(jnp.logical_and(tie, s_pos < cand))
            return jnp.where(c < need, cand, x)

        x = lax.fori_loop(0, nbits, idx_step, jnp.zeros((1, qg), jnp.int32))
        sel = jnp.logical_or(gt, jnp.logical_and(tie, s_pos <= x))
        mask_ref[...] = jnp.where(jnp.logical_and(sel, causal), 1, 0).astype(mask_ref.dtype)


def _indexer(proj3, col_qi, col_small, kidx3, grp, n_idx, wi_lane, topk):
    B, S, di = kidx3.shape
    wq = n_idx * di
    lk = (grp + 1) * Q_GROUP
    kern = functools.partial(_indexer_kernel, n_idx=n_idx, wi_lane=wi_lane, topk=topk,
                             q0=grp * Q_GROUP)
    return pl.pallas_call(
        kern,
        grid=(B,),
        in_specs=[pl.BlockSpec((None, Q_GROUP, wq), lambda b: (b, grp, col_qi // wq)),
                  pl.BlockSpec((None, Q_GROUP, LANES), lambda b: (b, grp, col_small // LANES)),
                  pl.BlockSpec((None, lk, di), lambda b: (b, 0, 0))],
        out_specs=pl.BlockSpec((None, lk, Q_GROUP), lambda b: (b, 0, 0)),
        out_shape=jax.ShapeDtypeStruct((B, lk, Q_GROUP), jnp.int8),
        scratch_shapes=[pltpu.VMEM((lk, Q_GROUP), F32),
                        pltpu.VMEM((lk, Q_GROUP), jnp.int32),
                        pltpu.VMEM((LANES, Q_GROUP), F32)],
        compiler_params=_cparams(("parallel",)),
        name="indexer",
    )(proj3, proj3, kidx3)


def _attn_kernel(qa_ref, za_ref, mask_ref, c_ref, ct_ref, wuk_ref, wuvt_ref, y_ref,
                 qlat_s, bias_s, s_s, p_s, *, heads, kt):
    lk, dc = c_ref.shape
    dh = wuk_ref.shape[2]
    qb = qa_ref.shape[0]
    R = heads * qb
    scale = (dh ** -0.5) * math.log2(math.e)
    nt = (((1,), (1,)), ((), ()))

    for g in range(heads):
        qh = qa_ref[:, g * dh:(g + 1) * dh].astype(BF16)
        ql = lax.dot_general(wuk_ref[g], qh, nt, preferred_element_type=F32)
        qlat_s[:, g * qb:(g + 1) * qb] = (ql * scale).astype(BF16)
    bias_s[...] = jnp.where(mask_ref[...].astype(jnp.int32) != 0, 0.0, -jnp.inf)

    def tile(i):
        return pl.ds(pl.multiple_of(i * kt, kt), kt)

    def pass1(i, m8):
        st = jnp.dot(c_ref[tile(i), :], qlat_s[...], preferred_element_type=F32)
        b = bias_s[tile(i), :]
        tops = []
        for g in range(heads):
            blk = st[:, g * qb:(g + 1) * qb] + b
            s_s[tile(i), g * qb:(g + 1) * qb] = blk
            tops.append(jnp.max(blk.reshape(kt // SUBLANES, SUBLANES, qb), axis=0))
        return jnp.maximum(m8, jnp.concatenate(tops, axis=1))

    m8 = lax.fori_loop(0, lk // kt, pass1, jnp.full((SUBLANES, R), -jnp.inf, F32))
    m = jnp.max(m8, axis=0, keepdims=True)

    def pass2(i, l8):
        p = jnp.exp2(s_s[tile(i), :] - m)
        p_s[tile(i), :] = p.astype(BF16)
        return l8 + jnp.sum(p.reshape(kt // SUBLANES, SUBLANES, R), axis=0)

    l8 = lax.fori_loop(0, lk // kt, pass2, jnp.zeros((SUBLANES, R), F32))
    l = jnp.sum(l8, axis=0, keepdims=True)

    olat = jnp.dot(ct_ref[...], p_s[...], preferred_element_type=F32)
    olat = (olat * (1.0 / l)).astype(BF16)
    for g in range(heads):
        ot = jnp.dot(wuvt_ref[g], olat[:, g * qb:(g + 1) * qb], preferred_element_type=F32)
        za = za_ref[:, g * dh:(g + 1) * dh]
        y_ref[:, g * dh:(g + 1) * dh] = (ot.T * (za * jax.nn.sigmoid(za))).astype(y_ref.dtype)


def _attention(proj3, col_qa, col_za, mask, c3, ct3, wuk, wuvt, grp, heads_per_step=8, kt=256):
    B, S, dc = c3.shape
    n_heads, _, dh = wuk.shape
    G = min(heads_per_step, n_heads)
    gw = G * dh
    lk = (grp + 1) * Q_GROUP
    nq = Q_GROUP // Q_BLOCK
    R = G * Q_BLOCK
    kern = functools.partial(_attn_kernel, heads=G, kt=min(kt, lk))
    return pl.pallas_call(
        kern,
        grid=(B, nq, n_heads // G),
        in_specs=[pl.BlockSpec((None, Q_BLOCK, gw), lambda b, q, g: (b, grp * nq + q, col_qa // gw + g)),
                  pl.BlockSpec((None, Q_BLOCK, gw), lambda b, q, g: (b, grp * nq + q, col_za // gw + g)),
                  pl.BlockSpec((None, lk, Q_BLOCK), lambda b, q, g: (b, 0, q)),
                  pl.BlockSpec((None, lk, dc), lambda b, q, g: (b, 0, 0)),
                  pl.BlockSpec((None, dc, lk), lambda b, q, g: (b, 0, 0)),
                  pl.BlockSpec((G, dc, dh), lambda b, q, g: (g, 0, 0)),
                  pl.BlockSpec((G, dh, dc), lambda b, q, g: (g, 0, 0))],
        out_specs=pl.BlockSpec((None, Q_BLOCK, gw), lambda b, q, g: (b, q, g)),
        out_shape=jax.ShapeDtypeStruct((B, Q_GROUP, n_heads * dh), BF16),
        scratch_shapes=[pltpu.VMEM((dc, R), BF16),
                        pltpu.VMEM((lk, Q_BLOCK), F32),
                        pltpu.VMEM((lk, R), F32),
                        pltpu.VMEM((lk, R), BF16)],
        compiler_params=_cparams(("parallel", "parallel", "parallel")),
        name="attention",
    )(proj3, proj3, mask, c3, ct3, wuk, wuvt)


def _blockdiag_dense(w):
    nblk, blk, _ = w.shape
    per = LANES // blk
    w4 = w.reshape(nblk // per, per, blk, blk)
    eye = jnp.eye(per, dtype=w.dtype)
    dense = jnp.einsum('tgio,gh->tgiho', w4, eye)
    return dense.reshape(nblk // per, LANES, LANES).astype(BF16)


def kernel(x, g_norm, w_in, conv_w, conv_b, w_q_m, w_k_m, w_v_m, b_i, b_f, g_head_m, skip_m, g_ckv, w_uk, w_uv, g_kidx, b_kidx, w_bm, w_ba, w_out, g_final):
    B, S, D = x.shape
    depth = g_norm.shape[0]
    T = B * S
    dm = conv_w.shape[-1]
    mh = b_i.shape[-1]
    ah, dc, adh = w_uk.shape[1:]
    da = ah * adh
    di = g_kidx.shape[-1]
    n_in = w_in.shape[-1]
    n_idx = (n_in - (3 * dm + 2 * mh + da + dc + da + di + 2 * D)) // (di + 1)
    topk = min(IDX_TOPK, S // 4)
    assert 2 * mh + n_idx <= LANES and S % Q_GROUP == 0

    sizes = (dm, dm, dm, mh, mh, da, dc, da, n_idx * di, di, n_idx, D, D)
    offs = [0]
    for s in sizes:
        offs.append(offs[-1] + s)
    (o_xm, o_om, o_zm, o_ip, o_fp, o_qa, o_ckv, o_za, o_qi, o_ki, o_wi, o_gm, o_ga) = offs[:-1]

    order = [(o_xm, dm), (o_om, dm), (o_zm, dm), (o_qa, da), (o_za, da), (o_qi, n_idx * di),
             (o_gm, D), (o_ga, D), (o_ckv, dc), (o_ki, di)]
    cols, pos = {}, 0
    for off, width in order:
        cols[off] = pos
        pos += width
    c_small = pos
    pos += LANES
    tn = 512
    n_pad = -(-pos // tn) * tn

    x2 = x.reshape(T, D)
    for l in range(depth):
        w = w_in[l]
        parts = [w[:, off:off + width] for off, width in order]
        parts += [w[:, o_ip:o_ip + mh], w[:, o_fp:o_fp + mh], w[:, o_wi:o_wi + n_idx],
                  jnp.zeros((D, n_pad - c_small - 2 * mh - n_idx), w.dtype)]
        w_cat = jnp.concatenate(parts, axis=1).astype(BF16)

        xn = _rmsnorm(x2, g_norm[l], BF16)
        proj = _matmul(xn, w_cat, F32, tn=tn)
        proj3 = proj.reshape(B, S, n_pad)

        gate_bias = jnp.concatenate([b_i[l], b_f[l], jnp.zeros((LANES - 2 * mh,), F32)]).reshape(1, LANES)
        y_m = _mlstm(proj3, cols[o_xm], cols[o_om], cols[o_zm], c_small, conv_w[l], conv_b[l],
                     _blockdiag_dense(w_q_m[l]), _blockdiag_dense(w_k_m[l]), _blockdiag_dense(w_v_m[l]),
                     gate_bias, g_head_m[l], skip_m[l], mh)

        c3, ct3, kidx3 = _dsa_prep(proj3, cols[o_ckv], cols[o_ki], g_ckv[l], g_kidx[l], b_kidx[l])
        wuk = w_uk[l].astype(BF16)
        wuvt = jnp.swapaxes(w_uv[l], 1, 2).astype(BF16)
        y_parts = []
        for grp in range(S // Q_GROUP):
            mask = _indexer(proj3, cols[o_qi], c_small, kidx3, grp, n_idx, 2 * mh, topk)
            y_parts.append(_attention(proj3, cols[o_qa], cols[o_za], mask, c3, ct3, wuk, wuvt, grp))
        y_a = jnp.concatenate(y_parts, axis=1)

        merged = _merge(y_m.reshape(T, dm), y_a.reshape(T, da), w_bm[l].astype(BF16),
                        w_ba[l].astype(BF16), proj, cols[o_gm], cols[o_ga])
        x2 = _out_proj(merged, w_out[l].astype(BF16), x2)
    out = _rmsnorm(x2, g_final, F32)
    return out.reshape(B, S, D)
```

```python
import functools
import math

import jax
import jax.numpy as jnp
from jax import lax
from jax.experimental import pallas as pl
from jax.experimental.pallas import tpu as pltpu

EPS = 1e-6
M_QKV_BLOCK = 4
M_CONV = 4
M_CHUNK = 64
IDX_TOPK = 256
Q_BLOCK = 128
Q_GROUP = 256
LANES = 128
SUBLANES = 8
VMEM_LIMIT = 56 * 1024 * 1024

F32 = jnp.float32
BF16 = jnp.bfloat16
INT_MIN = -2 ** 31


def _cparams(sem):
    return pltpu.CompilerParams(dimension_semantics=sem, vmem_limit_bytes=VMEM_LIMIT)


def _rmsnorm_kernel(x_ref, g_ref, o_ref):
    x = x_ref[...]
    y = x * lax.rsqrt(jnp.mean(x * x, axis=-1, keepdims=True) + EPS)
    o_ref[...] = (y * g_ref[...]).astype(o_ref.dtype)


def _rmsnorm(x2d, g, out_dtype, tm=256):
    T, D = x2d.shape
    tm = min(tm, T)
    return pl.pallas_call(
        _rmsnorm_kernel,
        grid=(T // tm,),
        in_specs=[pl.BlockSpec((tm, D), lambda i: (i, 0)),
                  pl.BlockSpec((1, D), lambda i: (0, 0))],
        out_specs=pl.BlockSpec((tm, D), lambda i: (i, 0)),
        out_shape=jax.ShapeDtypeStruct((T, D), out_dtype),
        compiler_params=_cparams(("parallel",)),
        name="rmsnorm",
    )(x2d, g.reshape(1, D))


def _cast_kernel(x_ref, o_ref):
    o_ref[...] = x_ref[...].astype(o_ref.dtype)


def _to_bf16(w, tm=512):
    M, N = w.shape
    tm = min(tm, M)
    return pl.pallas_call(
        _cast_kernel,
        grid=(M // tm,),
        in_specs=[pl.BlockSpec((tm, N), lambda i: (i, 0))],
        out_specs=pl.BlockSpec((tm, N), lambda i: (i, 0)),
        out_shape=jax.ShapeDtypeStruct((M, N), BF16),
        compiler_params=_cparams(("parallel",)),
        name="cast_bf16",
    )(w)


def _mm_kernel(a_ref, w_ref, o_ref):
    o_ref[...] = jnp.dot(a_ref[...], w_ref[...], preferred_element_type=F32).astype(o_ref.dtype)


def _matmul(a, w, out_dtype, tm=1024, tn=512):
    M, K = a.shape
    N = w.shape[1]
    tm, tn = min(tm, M), min(tn, N)
    return pl.pallas_call(
        _mm_kernel,
        grid=(M // tm, N // tn),
        in_specs=[pl.BlockSpec((tm, K), lambda i, j: (i, 0)),
                  pl.BlockSpec((K, tn), lambda i, j: (0, j))],
        out_specs=pl.BlockSpec((tm, tn), lambda i, j: (i, j)),
        out_shape=jax.ShapeDtypeStruct((M, N), out_dtype),
        compiler_params=_cparams(("parallel", "parallel")),
        name="in_proj",
    )(a, w)


def _merge_kernel(ym_ref, ya_ref, wbm_ref, wba_ref, gm_ref, ga_ref, o_ref):
    pm = jnp.dot(ym_ref[...], wbm_ref[...], preferred_element_type=F32)
    pa = jnp.dot(ya_ref[...], wba_ref[...], preferred_element_type=F32)
    o = jax.nn.sigmoid(gm_ref[...]) * pm + jax.nn.sigmoid(ga_ref[...]) * pa
    o_ref[...] = o.astype(o_ref.dtype)


def _merge(ym, ya, wbm, wba, proj, gm_col, ga_col, tm=512, tn=512):
    M, K = ym.shape
    N = wbm.shape[1]
    tm, tn = min(tm, M), min(tn, N)
    gm_blk, ga_blk = gm_col // tn, ga_col // tn
    return pl.pallas_call(
        _merge_kernel,
        grid=(M // tm, N // tn),
        in_specs=[pl.BlockSpec((tm, K), lambda i, j: (i, 0)),
                  pl.BlockSpec((tm, K), lambda i, j: (i, 0)),
                  pl.BlockSpec((K, tn), lambda i, j: (0, j)),
                  pl.BlockSpec((K, tn), lambda i, j: (0, j)),
                  pl.BlockSpec((tm, tn), lambda i, j: (i, gm_blk + j)),
                  pl.BlockSpec((tm, tn), lambda i, j: (i, ga_blk + j))],
        out_specs=pl.BlockSpec((tm, tn), lambda i, j: (i, j)),
        out_shape=jax.ShapeDtypeStruct((M, N), BF16),
        compiler_params=_cparams(("parallel", "parallel")),
        name="merge",
    )(ym, ya, wbm, wba, proj, proj)


def _out_kernel(a_ref, w_ref, x_ref, o_ref):
    o_ref[...] = x_ref[...] + jnp.dot(a_ref[...], w_ref[...], preferred_element_type=F32)


def _out_proj(a, w, x2d, tm=1024, tn=512):
    M, K = a.shape
    N = w.shape[1]
    tm, tn = min(tm, M), min(tn, N)
    return pl.pallas_call(
        _out_kernel,
        grid=(M // tm, N // tn),
        in_specs=[pl.BlockSpec((tm, K), lambda i, j: (i, 0)),
                  pl.BlockSpec((K, tn), lambda i, j: (0, j)),
                  pl.BlockSpec((tm, tn), lambda i, j: (i, j))],
        out_specs=pl.BlockSpec((tm, tn), lambda i, j: (i, j)),
        out_shape=jax.ShapeDtypeStruct((M, N), F32),
        compiler_params=_cparams(("parallel", "parallel")),
        name="out_proj",
    )(a, w, x2d)


def _log_sigmoid(x):
    return -(jnp.maximum(-x, 0.0) + jnp.log1p(jnp.exp(-jnp.abs(x))))


def _mlstm_kernel(xm_ref, om_ref, zm_ref, sm_ref, cw_ref, cb_ref, wq_ref, wk_ref, wv_ref,
                  gb_ref, gh_ref, sk_ref, y_ref,
                  xbuf, xc_s, q_s, k_s, v_s, c_s, n_s, m_s, *, n_heads, rows):
    h = pl.program_id(1)
    j = pl.program_id(2)
    dh = xc_s.shape[1]
    L = M_CHUNK
    pad = SUBLANES

    @pl.when(j == 0)
    def _():
        xbuf[0:pad, :] = jnp.zeros((pad, dh), F32)
        c_s[...] = jnp.zeros_like(c_s)
        n_s[...] = jnp.zeros_like(n_s)
        m_s[...] = jnp.zeros_like(m_s)

    xblk = xm_ref[...]
    xbuf[pad:pad + rows, :] = xblk
    acc = jnp.zeros((rows, dh), F32) + cb_ref[...]
    for k in range(M_CONV):
        acc = acc + cw_ref[k:k + 1, :] * xbuf[pl.ds(pad - (M_CONV - 1) + k, rows), :]
    xc = acc * jax.nn.sigmoid(acc)
    xc_s[...] = xc
    xbuf[0:pad, :] = xblk[rows - pad:rows, :]

    scale = dh ** -0.5
    for t in range(dh // LANES):
        sl = slice(t * LANES, (t + 1) * LANES)
        xct = xc[:, sl].astype(BF16)
        xmt = xblk[:, sl].astype(BF16)
        q_s[:, sl] = jnp.dot(xct, wq_ref[t], preferred_element_type=F32).astype(BF16)
        k_s[:, sl] = jnp.dot(xct, wk_ref[t], preferred_element_type=F32) * scale
        v_s[:, sl] = jnp.dot(xmt, wv_ref[t], preferred_element_type=F32).astype(BF16)

    lane = lax.broadcasted_iota(jnp.int32, (L, LANES), 1)
    row_i = lax.broadcasted_iota(jnp.int32, (L, L), 0)
    col_i = lax.broadcasted_iota(jnp.int32, (L, L), 1)
    tril = col_i <= row_i
    eye = col_i == row_i

    def to_row(col):
        return jnp.sum(jnp.where(eye, col, 0.0), axis=0, keepdims=True)

    def chunk(c, carry):
        r0 = pl.multiple_of(c * L, L)
        rs = pl.ds(r0, L)
        g = sm_ref[rs, :] + gb_ref[...]
        i_col = jnp.sum(jnp.where(lane == h, g, 0.0), axis=1, keepdims=True)
        f_col = _log_sigmoid(jnp.sum(jnp.where(lane == n_heads + h, g, 0.0), axis=1, keepdims=True))
        i_row = to_row(i_col)
        f_row = to_row(f_col)
        b_col = jnp.sum(jnp.where(tril, f_row, 0.0), axis=1, keepdims=True)
        b_row = jnp.sum(jnp.where(col_i >= row_i, f_col, 0.0), axis=0, keepdims=True)
        m_prev = m_s[...]
        dmat = jnp.where(tril, b_col - b_row + i_row, -jnp.inf)
        m_inter = b_col + m_prev
        m_t = jnp.maximum(m_inter, jnp.max(dmat, axis=1, keepdims=True))
        qc = q_s[rs, :]
        kc = k_s[rs, :]
        vc = v_s[rs, :]
        s_qk = lax.dot_general(qc, kc.astype(BF16), (((1,), (1,)), ((), ())),
                               preferred_element_type=F32)
        a = s_qk * jnp.exp(dmat - m_t)
        inter = jnp.exp(m_inter - m_t)
        c_prev = c_s[...]
        num = (jnp.dot(a.astype(BF16), vc, preferred_element_type=F32)
               + inter * jnp.dot(qc, c_prev.astype(BF16), preferred_element_type=F32))
        qn = jnp.sum(qc.astype(F32) * n_s[...], axis=1, keepdims=True)
        den = jnp.sum(a, axis=1, keepdims=True) + inter * qn
        hh = num / jnp.maximum(jnp.abs(den), jnp.exp(-m_t))

        mu = jnp.mean(hh, axis=1, keepdims=True)
        hc = hh - mu
        hn = hc * lax.rsqrt(jnp.mean(hc * hc, axis=1, keepdims=True) + EPS) * gh_ref[...]
        hn = jax.nn.sigmoid(om_ref[rs, :]) * hn
        zm = zm_ref[rs, :]
        y = (hn + sk_ref[...] * xc_s[rs, :]) * (zm * jax.nn.sigmoid(zm))
        y_ref[rs, :] = y.astype(y_ref.dtype)

        b_last = b_col[L - 1:L, :]
        g_col = b_last - b_col + i_col
        m_new = jnp.maximum(b_last + m_prev, jnp.max(g_col, axis=0, keepdims=True))
        w_col = jnp.exp(g_col - m_new)
        decay = jnp.exp(b_last + m_prev - m_new)
        kw = kc * w_col
        c_s[...] = decay * c_prev + lax.dot_general(
            kw.astype(BF16), vc, (((0,), (0,)), ((), ())), preferred_element_type=F32)
        n_s[...] = decay * n_s[...] + jnp.sum(kw, axis=0, keepdims=True)
        m_s[...] = m_new
        return carry

    lax.fori_loop(0, rows // L, chunk, 0, unroll=True)


def _mlstm(proj3, col_xm, col_om, col_zm, col_small, conv_w, conv_b, wq_d, wk_d, wv_d,
           gate_bias, g_head, skip, n_heads, rows=256):
    B, S, _ = proj3.shape
    dm = conv_w.shape[1]
    dh = dm // n_heads
    rows = min(rows, S)
    nt = dh // LANES
    bx, bo, bz, bs = col_xm // dh, col_om // dh, col_zm // dh, col_small // LANES
    seq = lambda base: pl.BlockSpec((None, rows, dh), lambda b, h, j: (b, j, base + h))
    vec = pl.BlockSpec((1, dh), lambda b, h, j: (0, h))
    wsp = pl.BlockSpec((nt, LANES, LANES), lambda b, h, j: (h, 0, 0))
    kern = functools.partial(_mlstm_kernel, n_heads=n_heads, rows=rows)
    return pl.pallas_call(
        kern,
        grid=(B, n_heads, S // rows),
        in_specs=[seq(bx), seq(bo), seq(bz),
                  pl.BlockSpec((None, rows, LANES), lambda b, h, j: (b, j, bs)),
                  pl.BlockSpec((M_CONV, dh), lambda b, h, j: (0, h)),
                  vec, wsp, wsp, wsp,
                  pl.BlockSpec((1, LANES), lambda b, h, j: (0, 0)),
                  vec, vec],
        out_specs=pl.BlockSpec((None, rows, dh), lambda b, h, j: (b, j, h)),
        out_shape=jax.ShapeDtypeStruct((B, S, dm), BF16),
        scratch_shapes=[pltpu.VMEM((rows + SUBLANES, dh), F32),
                        pltpu.VMEM((rows, dh), F32),
                        pltpu.VMEM((rows, dh), BF16),
                        pltpu.VMEM((rows, dh), F32),
                        pltpu.VMEM((rows, dh), BF16),
                        pltpu.VMEM((dh, dh), F32),
                        pltpu.VMEM((1, dh), F32),
                        pltpu.VMEM((1, 1), F32)],
        compiler_params=_cparams(("parallel", "parallel", "arbitrary")),
        name="mlstm",
    )(proj3, proj3, proj3, proj3, conv_w, conv_b.reshape(1, dm), wq_d, wk_d, wv_d,
      gate_bias, g_head.reshape(1, dm), skip.reshape(1, dm))


def _dsa_prep_kernel(ckv_ref, ki_ref, gc_ref, gk_ref, bk_ref, c_ref, ct_ref, kidx_ref):
    x = ckv_ref[...]
    c = x * lax.rsqrt(jnp.mean(x * x, axis=-1, keepdims=True) + EPS) * gc_ref[...]
    c_ref[...] = c.astype(c_ref.dtype)
    ct_ref[...] = c.T.astype(ct_ref.dtype)
    k = ki_ref[...]
    mu = jnp.mean(k, axis=-1, keepdims=True)
    kc = k - mu
    y = kc * lax.rsqrt(jnp.mean(kc * kc, axis=-1, keepdims=True) + EPS)
    kidx_ref[...] = (y * gk_ref[...] + bk_ref[...]).astype(kidx_ref.dtype)


def _dsa_prep(proj3, col_ckv, col_ki, g_ckv, g_kidx, b_kidx, tm=512):
    B, S, _ = proj3.shape
    dc, di = g_ckv.shape[0], g_kidx.shape[0]
    tm = min(tm, S)
    return pl.pallas_call(
        _dsa_prep_kernel,
        grid=(B, S // tm),
        in_specs=[pl.BlockSpec((None, tm, dc), lambda b, i: (b, i, col_ckv // dc)),
                  pl.BlockSpec((None, tm, di), lambda b, i: (b, i, col_ki // di)),
                  pl.BlockSpec((1, dc), lambda b, i: (0, 0)),
                  pl.BlockSpec((1, di), lambda b, i: (0, 0)),
                  pl.BlockSpec((1, di), lambda b, i: (0, 0))],
        out_specs=[pl.BlockSpec((None, tm, dc), lambda b, i: (b, i, 0)),
                   pl.BlockSpec((None, dc, tm), lambda b, i: (b, 0, i)),
                   pl.BlockSpec((None, tm, di), lambda b, i: (b, i, 0))],
        out_shape=[jax.ShapeDtypeStruct((B, S, dc), BF16),
                   jax.ShapeDtypeStruct((B, dc, S), BF16),
                   jax.ShapeDtypeStruct((B, S, di), BF16)],
        compiler_params=_cparams(("parallel", "parallel")),
        name="dsa_prep",
    )(proj3, proj3, g_ckv.reshape(1, dc), g_kidx.reshape(1, di), b_kidx.reshape(1, di))


def _indexer_kernel(qi_ref, sm_ref, kidx_ref, mask_ref, score_s, key_s, wt_s,
                    *, n_idx, wi_lane, topk, q0):
    lk, di = kidx_ref.shape
    qg = qi_ref.shape[0]
    kf = float(topk)
    wt_s[...] = sm_ref[...].T * ((n_idx ** -0.5) * (di ** -0.5))
    kidx = kidx_ref[...]
    score_s[...] = jnp.zeros_like(score_s)

    def head(hh, carry):
        c0 = pl.multiple_of(hh * di, di)
        qh = qi_ref[:, pl.ds(c0, di)].astype(BF16)
        logits = lax.dot_general(kidx, qh, (((1,), (1,)), ((), ())), preferred_element_type=F32)
        w = wt_s[pl.ds(wi_lane + hh, 1), :]
        score_s[...] += jnp.maximum(logits, 0.0) * w
        return carry

    lax.fori_loop(0, n_idx, head, 0, unroll=2)

    bits = pltpu.bitcast(score_s[...], jnp.int32)
    key = jnp.where(bits >= 0, bits, bits ^ jnp.int32(0x7FFFFFFF))
    s_pos = lax.broadcasted_iota(jnp.int32, (lk, qg), 0)
    t_pos = q0 + lax.broadcasted_iota(jnp.int32, (lk, qg), 1)
    causal = s_pos <= t_pos
    key_s[...] = jnp.where(causal, key, jnp.int32(INT_MIN))

    def count(pred):
        return jnp.sum(jnp.where(pred, 1.0, 0.0), axis=0, keepdims=True)

    thr = jnp.where(count(key_s[...] >= 0) >= kf, jnp.int32(0), jnp.int32(INT_MIN))

    def bit_step(it, thr):
        cand = thr + jnp.left_shift(jnp.int32(1), 30 - it)
        return jnp.where(count(key_s[...] >= cand) >= kf, cand, thr)

    thr = lax.fori_loop(0, 31, bit_step, thr)
    key = key_s[...]
    ge = key >= thr
    mask_ref[...] = jnp.where(jnp.logical_and(ge, causal), 1, 0).astype(mask_ref.dtype)

    excess = jnp.logical_and(count(ge) > kf, thr > jnp.int32(INT_MIN))

    @pl.when(jnp.max(jnp.where(excess, 1.0, 0.0)) > 0.0)
    def _():
        gt = key > thr
        tie = key == thr
        need = kf - count(gt)
        nbits = max(1, (lk - 1).bit_length())

        def idx_step(it, x):
            cand = x + jnp.left_shift(jnp.int32(1), nbits - 1 - it)
            c = count(jnp.logical_and(tie, s_pos < cand))
            return jnp.where(c < need, cand, x)

        x = lax.fori_loop(0, nbits, idx_step, jnp.zeros((1, qg), jnp.int32))
        sel = jnp.logical_or(gt, jnp.logical_and(tie, s_pos <= x))
        mask_ref[...] = jnp.where(jnp.logical_and(sel, causal), 1, 0).astype(mask_ref.dtype)


def _indexer(proj3, col_qi, col_small, kidx3, grp, n_idx, wi_lane, topk):
    B, S, di = kidx3.shape
    wq = n_idx * di
    lk = (grp + 1) * Q_GROUP
    kern = functools.partial(_indexer_kernel, n_idx=n_idx, wi_lane=wi_lane, topk=topk,
                             q0=grp * Q_GROUP)
    return pl.pallas_call(
        kern,
        grid=(B,),
        in_specs=[pl.BlockSpec((None, Q_GROUP, wq), lambda b: (b, grp, col_qi // wq)),
                  pl.BlockSpec((None, Q_GROUP, LANES), lambda b: (b, grp, col_small // LANES)),
                  pl.BlockSpec((None, lk, di), lambda b: (b, 0, 0))],
        out_specs=pl.BlockSpec((None, lk, Q_GROUP), lambda b: (b, 0, 0)),
        out_shape=jax.ShapeDtypeStruct((B, lk, Q_GROUP), jnp.int8),
        scratch_shapes=[pltpu.VMEM((lk, Q_GROUP), F32),
                        pltpu.VMEM((lk, Q_GROUP), jnp.int32),
                        pltpu.VMEM((LANES, Q_GROUP), F32)],
        compiler_params=_cparams(("parallel",)),
        name="indexer",
    )(proj3, proj3, kidx3)


def _attn_kernel(qa_ref, za_ref, mask_ref, c_ref, ct_ref, wuk_ref, wuvt_ref, y_all_ref, y_ref,
                 qlat_s, bias_s, s_s, *, heads, kt):
    del y_all_ref
    lk, dc = c_ref.shape
    dh = wuk_ref.shape[2]
    qb = qa_ref.shape[0]
    R = heads * qb
    assert lk % kt == 0 and kt % SUBLANES == 0
    scale = (dh ** -0.5) * math.log2(math.e)
    nt = (((1,), (1,)), ((), ()))

    for g in range(heads):
        qh = qa_ref[:, g * dh:(g + 1) * dh].astype(BF16)
        ql = lax.dot_general(wuk_ref[g], qh, nt, preferred_element_type=F32)
        qlat_s[:, g * qb:(g + 1) * qb] = (ql * scale).astype(BF16)
    bias_s[...] = jnp.where(mask_ref[...].astype(jnp.int32) != 0, 0.0, -jnp.inf)

    m8 = jnp.full((SUBLANES, R), -jnp.inf, F32)
    for i in range(lk // kt):
        ks = slice(i * kt, (i + 1) * kt)
        st = jnp.dot(c_ref[ks, :], qlat_s[...], preferred_element_type=F32)
        b = bias_s[ks, :]
        tops = []
        for g in range(heads):
            blk = st[:, g * qb:(g + 1) * qb] + b
            s_s[ks, g * qb:(g + 1) * qb] = blk
            tops.append(jnp.max(blk.reshape(kt // SUBLANES, SUBLANES, qb), axis=0))
        m8 = jnp.maximum(m8, jnp.concatenate(tops, axis=1))
    m = jnp.max(m8, axis=0, keepdims=True)

    l8 = jnp.zeros((SUBLANES, R), F32)
    olat = jnp.zeros((dc, R), F32)
    for i in range(lk // kt):
        ks = slice(i * kt, (i + 1) * kt)
        p = jnp.exp2(s_s[ks, :] - m)
        l8 = l8 + jnp.sum(p.reshape(kt // SUBLANES, SUBLANES, R), axis=0)
        olat = olat + jnp.dot(ct_ref[:, ks], p.astype(BF16), preferred_element_type=F32)
    l = jnp.sum(l8, axis=0, keepdims=True)
    olat = (olat * (1.0 / l)).astype(BF16)
    for g in range(heads):
        ot = jnp.dot(wuvt_ref[g], olat[:, g * qb:(g + 1) * qb], preferred_element_type=F32)
        za = za_ref[:, g * dh:(g + 1) * dh]
        y_ref[:, g * dh:(g + 1) * dh] = (ot.T * (za * jax.nn.sigmoid(za))).astype(y_ref.dtype)


def _attention(proj3, col_qa, col_za, mask, c3, ct3, wuk, wuvt, y_all, grp, heads_per_step=8, kt=512):
    B, S, dc = c3.shape
    n_heads, _, dh = wuk.shape
    G = min(heads_per_step, n_heads)
    gw = G * dh
    lk = (grp + 1) * Q_GROUP
    nq = Q_GROUP // Q_BLOCK
    R = G * Q_BLOCK
    kt = math.gcd(kt, lk)
    kern = functools.partial(_attn_kernel, heads=G, kt=kt)
    return pl.pallas_call(
        kern,
        grid=(B, nq, n_heads // G),
        in_specs=[pl.BlockSpec((None, Q_BLOCK, gw), lambda b, q, g: (b, grp * nq + q, col_qa // gw + g)),
                  pl.BlockSpec((None, Q_BLOCK, gw), lambda b, q, g: (b, grp * nq + q, col_za // gw + g)),
                  pl.BlockSpec((None, lk, Q_BLOCK), lambda b, q, g: (b, 0, q)),
                  pl.BlockSpec((None, lk, dc), lambda b, q, g: (b, 0, 0)),
                  pl.BlockSpec((None, dc, lk), lambda b, q, g: (b, 0, 0)),
                  pl.BlockSpec((G, dc, dh), lambda b, q, g: (g, 0, 0)),
                  pl.BlockSpec((G, dh, dc), lambda b, q, g: (g, 0, 0)),
                  pl.BlockSpec(memory_space=pl.ANY)],
        out_specs=pl.BlockSpec((None, Q_BLOCK, gw), lambda b, q, g: (b, grp * nq + q, g)),
        out_shape=jax.ShapeDtypeStruct(y_all.shape, y_all.dtype),
        input_output_aliases={7: 0},
        scratch_shapes=[pltpu.VMEM((dc, R), BF16),
                        pltpu.VMEM((lk, Q_BLOCK), F32),
                        pltpu.VMEM((lk, R), F32)],
        compiler_params=_cparams(("parallel", "parallel", "parallel")),
        name="attention",
    )(proj3, proj3, mask, c3, ct3, wuk, wuvt, y_all)


def _blockdiag_dense(w):
    nblk, blk, _ = w.shape
    per = LANES // blk
    w4 = w.reshape(nblk // per, per, blk, blk)
    eye = jnp.eye(per, dtype=w.dtype)
    dense = jnp.einsum('tgio,gh->tgiho', w4, eye)
    return dense.reshape(nblk // per, LANES, LANES).astype(BF16)


def kernel(x, g_norm, w_in, conv_w, conv_b, w_q_m, w_k_m, w_v_m, b_i, b_f, g_head_m, skip_m, g_ckv, w_uk, w_uv, g_kidx, b_kidx, w_bm, w_ba, w_out, g_final):
    B, S, D = x.shape
    depth = g_norm.shape[0]
    T = B * S
    dm = conv_w.shape[-1]
    mh = b_i.shape[-1]
    ah, dc, adh = w_uk.shape[1:]
    da = ah * adh
    di = g_kidx.shape[-1]
    n_in = w_in.shape[-1]
    n_idx = (n_in - (3 * dm + 2 * mh + da + dc + da + di + 2 * D)) // (di + 1)
    topk = min(IDX_TOPK, S // 4)
    assert 2 * mh + n_idx <= LANES and S % Q_GROUP == 0

    sizes = (dm, dm, dm, mh, mh, da, dc, da, n_idx * di, di, n_idx, D, D)
    offs = [0]
    for s in sizes:
        offs.append(offs[-1] + s)
    (o_xm, o_om, o_zm, o_ip, o_fp, o_qa, o_ckv, o_za, o_qi, o_ki, o_wi, o_gm, o_ga) = offs[:-1]

    order = [(o_xm, dm), (o_om, dm), (o_zm, dm), (o_qa, da), (o_za, da), (o_qi, n_idx * di),
             (o_gm, D), (o_ga, D), (o_ckv, dc), (o_ki, di)]
    cols, pos = {}, 0
    for off, width in order:
        cols[off] = pos
        pos += width
    c_small = pos
    pos += LANES
    tn = 512
    n_pad = -(-pos // tn) * tn

    x2 = x.reshape(T, D)
    for l in range(depth):
        w = w_in[l]
        parts = [w[:, off:off + width] for off, width in order]
        parts += [w[:, o_ip:o_ip + mh], w[:, o_fp:o_fp + mh], w[:, o_wi:o_wi + n_idx],
                  jnp.zeros((D, n_pad - c_small - 2 * mh - n_idx), w.dtype)]
        w_cat = jnp.concatenate(parts, axis=1).astype(BF16)

        xn = _rmsnorm(x2, g_norm[l], BF16)
        proj = _matmul(xn, w_cat, F32, tn=tn)
        proj3 = proj.reshape(B, S, n_pad)

        gate_bias = jnp.concatenate([b_i[l], b_f[l], jnp.zeros((LANES - 2 * mh,), F32)]).reshape(1, LANES)
        y_m = _mlstm(proj3, cols[o_xm], cols[o_om], cols[o_zm], c_small, conv_w[l], conv_b[l],
                     _blockdiag_dense(w_q_m[l]), _blockdiag_dense(w_k_m[l]), _blockdiag_dense(w_v_m[l]),
                     gate_bias, g_head_m[l], skip_m[l], mh)

        c3, ct3, kidx3 = _dsa_prep(proj3, cols[o_ckv], cols[o_ki], g_ckv[l], g_kidx[l], b_kidx[l])
        wuk = w_uk[l].astype(BF16)
        wuvt = jnp.swapaxes(w_uv[l], 1, 2).astype(BF16)
        y_a = jnp.zeros((B, S, da), BF16)
        for grp in range(S // Q_GROUP):
            mask = _indexer(proj3, cols[o_qi], c_small, kidx3, grp, n_idx, 2 * mh, topk)
            y_a = _attention(proj3, cols[o_qa], cols[o_za], mask, c3, ct3, wuk, wuvt, y_a, grp)

        merged = _merge(y_m.reshape(T, dm), y_a.reshape(T, da), _to_bf16(w_bm[l]),
                        _to_bf16(w_ba[l]), proj, cols[o_gm], cols[o_ga])
        x2 = _out_proj(merged, _to_bf16(w_out[l]), x2)
    out = _rmsnorm(x2, g_final, F32)
    return out.reshape(B, S, D)
```

```python
import functools
import math

import jax
import jax.numpy as jnp
from jax import lax
from jax.experimental import pallas as pl
from jax.experimental.pallas import tpu as pltpu

EPS = 1e-6
M_QKV_BLOCK = 4
M_CONV = 4
IDX_TOPK = 256
Q_BLOCK = 128
Q_GROUP = 256
LANES = 128
SUBLANES = 8
VMEM_LIMIT = 56 * 1024 * 1024

F32 = jnp.float32
BF16 = jnp.bfloat16
INT_MIN = -2 ** 31


def _cparams(sem):
    return pltpu.CompilerParams(dimension_semantics=sem, vmem_limit_bytes=VMEM_LIMIT)


def _rmsnorm_kernel(x_ref, g_ref, o_ref):
    x = x_ref[...]
    y = x * lax.rsqrt(jnp.mean(x * x, axis=-1, keepdims=True) + EPS)
    o_ref[...] = (y * g_ref[...]).astype(o_ref.dtype)


def _rmsnorm(x2d, g, out_dtype, tm=256):
    T, D = x2d.shape
    tm = min(tm, T)
    return pl.pallas_call(
        _rmsnorm_kernel,
        grid=(T // tm,),
        in_specs=[pl.BlockSpec((tm, D), lambda i: (i, 0)),
                  pl.BlockSpec((1, D), lambda i: (0, 0))],
        out_specs=pl.BlockSpec((tm, D), lambda i: (i, 0)),
        out_shape=jax.ShapeDtypeStruct((T, D), out_dtype),
        compiler_params=_cparams(("parallel",)),
        name="rmsnorm",
    )(x2d, g.reshape(1, D))


def _cast_kernel(x_ref, o_ref):
    o_ref[...] = x_ref[...].astype(o_ref.dtype)


def _to_bf16(w, tm=512):
    M, N = w.shape
    tm = min(tm, M)
    return pl.pallas_call(
        _cast_kernel,
        grid=(M // tm,),
        in_specs=[pl.BlockSpec((tm, N), lambda i: (i, 0))],
        out_specs=pl.BlockSpec((tm, N), lambda i: (i, 0)),
        out_shape=jax.ShapeDtypeStruct((M, N), BF16),
        compiler_params=_cparams(("parallel",)),
        name="cast_bf16",
    )(w)


def _mm_kernel(a_ref, w_ref, o_ref):
    o_ref[...] = jnp.dot(a_ref[...], w_ref[...], preferred_element_type=F32).astype(o_ref.dtype)


def _matmul(a, w, out_dtype, tm=1024, tn=512):
    M, K = a.shape
    N = w.shape[1]
    tm, tn = min(tm, M), min(tn, N)
    return pl.pallas_call(
        _mm_kernel,
        grid=(M // tm, N // tn),
        in_specs=[pl.BlockSpec((tm, K), lambda i, j: (i, 0)),
                  pl.BlockSpec((K, tn), lambda i, j: (0, j))],
        out_specs=pl.BlockSpec((tm, tn), lambda i, j: (i, j)),
        out_shape=jax.ShapeDtypeStruct((M, N), out_dtype),
        compiler_params=_cparams(("parallel", "parallel")),
        name="in_proj",
    )(a, w)


def _merge_kernel(ym_ref, ya_ref, wbm_ref, wba_ref, gm_ref, ga_ref, o_ref):
    pm = jnp.dot(ym_ref[...], wbm_ref[...], preferred_element_type=F32)
    pa = jnp.dot(ya_ref[...], wba_ref[...], preferred_element_type=F32)
    o = jax.nn.sigmoid(gm_ref[...]) * pm + jax.nn.sigmoid(ga_ref[...]) * pa
    o_ref[...] = o.astype(o_ref.dtype)


def _merge(ym, ya, wbm, wba, proj, gm_col, ga_col, tm=512, tn=512):
    M, K = ym.shape
    N = wbm.shape[1]
    tm, tn = min(tm, M), min(tn, N)
    gm_blk, ga_blk = gm_col // tn, ga_col // tn
    return pl.pallas_call(
        _merge_kernel,
        grid=(M // tm, N // tn),
        in_specs=[pl.BlockSpec((tm, K), lambda i, j: (i, 0)),
                  pl.BlockSpec((tm, K), lambda i, j: (i, 0)),
                  pl.BlockSpec((K, tn), lambda i, j: (0, j)),
                  pl.BlockSpec((K, tn), lambda i, j: (0, j)),
                  pl.BlockSpec((tm, tn), lambda i, j: (i, gm_blk + j)),
                  pl.BlockSpec((tm, tn), lambda i, j: (i, ga_blk + j))],
        out_specs=pl.BlockSpec((tm, tn), lambda i, j: (i, j)),
        out_shape=jax.ShapeDtypeStruct((M, N), BF16),
        compiler_params=_cparams(("parallel", "parallel")),
        name="merge",
    )(ym, ya, wbm, wba, proj, proj)


def _out_kernel(a_ref, w_ref, x_ref, o_ref):
    o_ref[...] = x_ref[...] + jnp.dot(a_ref[...], w_ref[...], preferred_element_type=F32)


def _out_proj(a, w, x2d, tm=1024, tn=512):
    M, K = a.shape
    N = w.shape[1]
    tm, tn = min(tm, M), min(tn, N)
    return pl.pallas_call(
        _out_kernel,
        grid=(M // tm, N // tn),
        in_specs=[pl.BlockSpec((tm, K), lambda i, j: (i, 0)),
                  pl.BlockSpec((K, tn), lambda i, j: (0, j)),
                  pl.BlockSpec((tm, tn), lambda i, j: (i, j))],
        out_specs=pl.BlockSpec((tm, tn), lambda i, j: (i, j)),
        out_shape=jax.ShapeDtypeStruct((M, N), F32),
        compiler_params=_cparams(("parallel", "parallel")),
        name="out_proj",
    )(a, w, x2d)


def _sigmoid(x):
    return 0.5 * jnp.tanh(0.5 * x) + 0.5


def _log_sigmoid(x):
    return -(jnp.maximum(-x, 0.0) + jnp.log1p(jnp.exp(-jnp.abs(x))))


def _mlstm_kernel(xm_ref, om_ref, zm_ref, sm_ref, cw_ref, cb_ref, wq_ref, wk_ref, wv_ref,
                  gb_ref, gh_ref, sk_ref, y_ref,
                  xbuf, xc_s, q_s, k_s, v_s, c_s, n_s, m_s, *, n_heads, rows, chunk):
    h = pl.program_id(1)
    j = pl.program_id(2)
    dh = xc_s.shape[1]
    L = chunk
    pad = SUBLANES

    @pl.when(j == 0)
    def _():
        xbuf[0:pad, :] = jnp.zeros((pad, dh), F32)
        c_s[...] = jnp.zeros_like(c_s)
        n_s[...] = jnp.zeros_like(n_s)
        m_s[...] = jnp.zeros_like(m_s)

    xblk = xm_ref[...]
    xbuf[pad:pad + rows, :] = xblk
    acc = jnp.zeros((rows, dh), F32) + cb_ref[...]
    for k in range(M_CONV):
        acc = acc + cw_ref[k:k + 1, :] * xbuf[pl.ds(pad - (M_CONV - 1) + k, rows), :]
    xc = acc * _sigmoid(acc)
    xc_s[...] = xc
    xbuf[0:pad, :] = xblk[rows - pad:rows, :]

    scale = dh ** -0.5
    for t in range(dh // LANES):
        sl = slice(t * LANES, (t + 1) * LANES)
        xct = xc[:, sl].astype(BF16)
        xmt = xblk[:, sl].astype(BF16)
        q_s[:, sl] = jnp.dot(xct, wq_ref[t], preferred_element_type=F32).astype(BF16)
        k_s[:, sl] = jnp.dot(xct, wk_ref[t], preferred_element_type=F32) * scale
        v_s[:, sl] = jnp.dot(xmt, wv_ref[t], preferred_element_type=F32).astype(BF16)

    lane = lax.broadcasted_iota(jnp.int32, (L, LANES), 1)
    row_i = lax.broadcasted_iota(jnp.int32, (L, L), 0)
    col_i = lax.broadcasted_iota(jnp.int32, (L, L), 1)
    tril = col_i <= row_i
    eye = col_i == row_i

    def to_row(col):
        return jnp.sum(jnp.where(eye, col, 0.0), axis=0, keepdims=True)

    def chunk(c, carry):
        r0 = pl.multiple_of(c * L, L)
        rs = pl.ds(r0, L)
        g = sm_ref[rs, :] + gb_ref[...]
        i_col = jnp.sum(jnp.where(lane == h, g, 0.0), axis=1, keepdims=True)
        f_col = _log_sigmoid(jnp.sum(jnp.where(lane == n_heads + h, g, 0.0), axis=1, keepdims=True))
        i_row = to_row(i_col)
        f_row = to_row(f_col)
        b_col = jnp.sum(jnp.where(tril, f_row, 0.0), axis=1, keepdims=True)
        b_row = jnp.sum(jnp.where(col_i >= row_i, f_col, 0.0), axis=0, keepdims=True)
        m_prev = m_s[...]
        dmat = jnp.where(tril, b_col - b_row + i_row, -jnp.inf)
        m_inter = b_col + m_prev
        m_t = jnp.maximum(m_inter, jnp.max(dmat, axis=1, keepdims=True))
        qc = q_s[rs, :]
        kc = k_s[rs, :]
        vc = v_s[rs, :]
        s_qk = lax.dot_general(qc, kc.astype(BF16), (((1,), (1,)), ((), ())),
                               preferred_element_type=F32)
        a = s_qk * jnp.exp(dmat - m_t)
        inter = jnp.exp(m_inter - m_t)
        c_prev = c_s[...]
        num = (jnp.dot(a.astype(BF16), vc, preferred_element_type=F32)
               + inter * jnp.dot(qc, c_prev.astype(BF16), preferred_element_type=F32))
        qn = jnp.sum(qc.astype(F32) * n_s[...], axis=1, keepdims=True)
        den = jnp.sum(a, axis=1, keepdims=True) + inter * qn
        hh = num / jnp.maximum(jnp.abs(den), jnp.exp(-m_t))

        mu = jnp.mean(hh, axis=1, keepdims=True)
        hc = hh - mu
        hn = hc * lax.rsqrt(jnp.mean(hc * hc, axis=1, keepdims=True) + EPS) * gh_ref[...]
        hn = _sigmoid(om_ref[rs, :]) * hn
        zm = zm_ref[rs, :]
        y = (hn + sk_ref[...] * xc_s[rs, :]) * (zm * _sigmoid(zm))
        y_ref[rs, :] = y.astype(y_ref.dtype)

        b_last = b_col[L - 1:L, :]
        g_col = b_last - b_col + i_col
        m_new = jnp.maximum(b_last + m_prev, jnp.max(g_col, axis=0, keepdims=True))
        w_col = jnp.exp(g_col - m_new)
        decay = jnp.exp(b_last + m_prev - m_new)
        kw = kc * w_col
        c_s[...] = decay * c_prev + lax.dot_general(
            kw.astype(BF16), vc, (((0,), (0,)), ((), ())), preferred_element_type=F32)
        n_s[...] = decay * n_s[...] + jnp.sum(kw, axis=0, keepdims=True)
        m_s[...] = m_new
        return carry

    lax.fori_loop(0, rows // L, chunk, 0, unroll=True)


def _mlstm(proj3, col_xm, col_om, col_zm, col_small, conv_w, conv_b, wq_d, wk_d, wv_d,
           gate_bias, g_head, skip, n_heads, rows=256, chunk=256):
    B, S, _ = proj3.shape
    dm = conv_w.shape[1]
    dh = dm // n_heads
    rows = min(rows, S)
    chunk = min(chunk, rows)
    assert rows % chunk == 0 and S % rows == 0
    nt = dh // LANES
    bx, bo, bz, bs = col_xm // dh, col_om // dh, col_zm // dh, col_small // LANES
    seq = lambda base: pl.BlockSpec((None, rows, dh), lambda b, h, j: (b, j, base + h))
    vec = pl.BlockSpec((1, dh), lambda b, h, j: (0, h))
    wsp = pl.BlockSpec((nt, LANES, LANES), lambda b, h, j: (h, 0, 0))
    kern = functools.partial(_mlstm_kernel, n_heads=n_heads, rows=rows, chunk=chunk)
    return pl.pallas_call(
        kern,
        grid=(B, n_heads, S // rows),
        in_specs=[seq(bx), seq(bo), seq(bz),
                  pl.BlockSpec((None, rows, LANES), lambda b, h, j: (b, j, bs)),
                  pl.BlockSpec((M_CONV, dh), lambda b, h, j: (0, h)),
                  vec, wsp, wsp, wsp,
                  pl.BlockSpec((1, LANES), lambda b, h, j: (0, 0)),
                  vec, vec],
        out_specs=pl.BlockSpec((None, rows, dh), lambda b, h, j: (b, j, h)),
        out_shape=jax.ShapeDtypeStruct((B, S, dm), BF16),
        scratch_shapes=[pltpu.VMEM((rows + SUBLANES, dh), F32),
                        pltpu.VMEM((rows, dh), F32),
                        pltpu.VMEM((rows, dh), BF16),
                        pltpu.VMEM((rows, dh), F32),
                        pltpu.VMEM((rows, dh), BF16),
                        pltpu.VMEM((dh, dh), F32),
                        pltpu.VMEM((1, dh), F32),
                        pltpu.VMEM((1, 1), F32)],
        compiler_params=_cparams(("parallel", "parallel", "arbitrary")),
        name="mlstm",
    )(proj3, proj3, proj3, proj3, conv_w, conv_b.reshape(1, dm), wq_d, wk_d, wv_d,
      gate_bias, g_head.reshape(1, dm), skip.reshape(1, dm))


def _dsa_prep_kernel(ckv_ref, ki_ref, gc_ref, gk_ref, bk_ref, c_ref, ct_ref, kidx_ref):
    x = ckv_ref[...]
    c = x * lax.rsqrt(jnp.mean(x * x, axis=-1, keepdims=True) + EPS) * gc_ref[...]
    c_ref[...] = c.astype(c_ref.dtype)
    ct_ref[...] = c.T.astype(ct_ref.dtype)
    k = ki_ref[...]
    mu = jnp.mean(k, axis=-1, keepdims=True)
    kc = k - mu
    y = kc * lax.rsqrt(jnp.mean(kc * kc, axis=-1, keepdims=True) + EPS)
    kidx_ref[...] = (y * gk_ref[...] + bk_ref[...]).astype(kidx_ref.dtype)


def _dsa_prep(proj3, col_ckv, col_ki, g_ckv, g_kidx, b_kidx, tm=512):
    B, S, _ = proj3.shape
    dc, di = g_ckv.shape[0], g_kidx.shape[0]
    tm = min(tm, S)
    return pl.pallas_call(
        _dsa_prep_kernel,
        grid=(B, S // tm),
        in_specs=[pl.BlockSpec((None, tm, dc), lambda b, i: (b, i, col_ckv // dc)),
                  pl.BlockSpec((None, tm, di), lambda b, i: (b, i, col_ki // di)),
                  pl.BlockSpec((1, dc), lambda b, i: (0, 0)),
                  pl.BlockSpec((1, di), lambda b, i: (0, 0)),
                  pl.BlockSpec((1, di), lambda b, i: (0, 0))],
        out_specs=[pl.BlockSpec((None, tm, dc), lambda b, i: (b, i, 0)),
                   pl.BlockSpec((None, dc, tm), lambda b, i: (b, 0, i)),
                   pl.BlockSpec((None, tm, di), lambda b, i: (b, i, 0))],
        out_shape=[jax.ShapeDtypeStruct((B, S, dc), BF16),
                   jax.ShapeDtypeStruct((B, dc, S), BF16),
                   jax.ShapeDtypeStruct((B, S, di), BF16)],
        compiler_params=_cparams(("parallel", "parallel")),
        name="dsa_prep",
    )(proj3, proj3, g_ckv.reshape(1, dc), g_kidx.reshape(1, di), b_kidx.reshape(1, di))


def _indexer_kernel(qi_ref, sm_ref, kidx_ref, mask_ref, score_s, key_s, wt_s,
                    *, n_idx, wi_lane, topk, q0):
    lk, di = kidx_ref.shape
    qg = qi_ref.shape[0]
    kf = float(topk)
    wt_s[...] = sm_ref[...].T * ((n_idx ** -0.5) * (di ** -0.5))
    kidx = kidx_ref[...]
    score_s[...] = jnp.zeros_like(score_s)

    def head(hh, carry):
        c0 = pl.multiple_of(hh * di, di)
        qh = qi_ref[:, pl.ds(c0, di)].astype(BF16)
        logits = lax.dot_general(kidx, qh, (((1,), (1,)), ((), ())), preferred_element_type=F32)
        w = wt_s[pl.ds(wi_lane + hh, 1), :]
        score_s[...] += jnp.maximum(logits, 0.0) * w
        return carry

    lax.fori_loop(0, n_idx, head, 0, unroll=2)

    bits = pltpu.bitcast(score_s[...], jnp.int32)
    key = jnp.where(bits >= 0, bits, bits ^ jnp.int32(0x7FFFFFFF))
    s_pos = lax.broadcasted_iota(jnp.int32, (lk, qg), 0)
    t_pos = q0 + lax.broadcasted_iota(jnp.int32, (lk, qg), 1)
    causal = s_pos <= t_pos
    key_s[...] = jnp.where(causal, key, jnp.int32(INT_MIN))

    def count(pred):
        return jnp.sum(jnp.where(pred, 1.0, 0.0), axis=0, keepdims=True)

    thr = jnp.where(count(key_s[...] >= 0) >= kf, jnp.int32(0), jnp.int32(INT_MIN))

    def bit_step(it, thr):
        cand = thr + jnp.left_shift(jnp.int32(1), 30 - it)
        return jnp.where(count(key_s[...] >= cand) >= kf, cand, thr)

    thr = lax.fori_loop(0, 31, bit_step, thr)
    key = key_s[...]
    ge = key >= thr
    mask_ref[...] = jnp.where(jnp.logical_and(ge, causal), 1, 0).astype(mask_ref.dtype)

    excess = jnp.logical_and(count(ge) > kf, thr > jnp.int32(INT_MIN))

    @pl.when(jnp.max(jnp.where(excess, 1.0, 0.0)) > 0.0)
    def _():
        gt = key > thr
        tie = key == thr
        need = kf - count(gt)
        nbits = max(1, (lk - 1).bit_length())

        def idx_step(it, x):
            cand = x + jnp.left_shift(jnp.int32(1), nbits - 1 - it)
            c = count(jnp.logical_and(tie, s_pos < cand))
            return jnp.where(c < need, cand, x)

        x = lax.fori_loop(0, nbits, idx_step, jnp.zeros((1, qg), jnp.int32))
        sel = jnp.logical_or(gt, jnp.logical_and(tie, s_pos <= x))
        mask_ref[...] = jnp.where(jnp.logical_and(sel, causal), 1, 0).astype(mask_ref.dtype)


def _indexer(proj3, col_qi, col_small, kidx3, grp, n_idx, wi_lane, topk):
    B, S, di = kidx3.shape
    wq = n_idx * di
    lk = (grp + 1) * Q_GROUP
    kern = functools.partial(_indexer_kernel, n_idx=n_idx, wi_lane=wi_lane, topk=topk,
                             q0=grp * Q_GROUP)
    return pl.pallas_call(
        kern,
        grid=(B,),
        in_specs=[pl.BlockSpec((None, Q_GROUP, wq), lambda b: (b, grp, col_qi // wq)),
                  pl.BlockSpec((None, Q_GROUP, LANES), lambda b: (b, grp, col_small // LANES)),
                  pl.BlockSpec((None, lk, di), lambda b: (b, 0, 0))],
        out_specs=pl.BlockSpec((None, lk, Q_GROUP), lambda b: (b, 0, 0)),
        out_shape=jax.ShapeDtypeStruct((B, lk, Q_GROUP), jnp.int8),
        scratch_shapes=[pltpu.VMEM((lk, Q_GROUP), F32),
                        pltpu.VMEM((lk, Q_GROUP), jnp.int32),
                        pltpu.VMEM((LANES, Q_GROUP), F32)],
        compiler_params=_cparams(("parallel",)),
        name="indexer",
    )(proj3, proj3, kidx3)


def _attn_kernel(qa_ref, za_ref, mask_ref, c_ref, ct_ref, wuk_ref, wuvt_ref, y_all_ref, y_ref,
                 qlat_s, bias_s, s_s, *, heads, kt):
    del y_all_ref
    lk, dc = c_ref.shape
    dh = wuk_ref.shape[2]
    qb = qa_ref.shape[0]
    R = heads * qb
    assert lk % kt == 0 and kt % SUBLANES == 0
    scale = (dh ** -0.5) * math.log2(math.e)
    nt = (((1,), (1,)), ((), ()))

    for g in range(heads):
        qh = qa_ref[:, g * dh:(g + 1) * dh].astype(BF16)
        ql = lax.dot_general(wuk_ref[g], qh, nt, preferred_element_type=F32)
        qlat_s[:, g * qb:(g + 1) * qb] = (ql * scale).astype(BF16)
    bias_s[...] = jnp.where(mask_ref[...].astype(jnp.int32) != 0, 0.0, -jnp.inf)

    m8 = jnp.full((SUBLANES, R), -jnp.inf, F32)
    for i in range(lk // kt):
        ks = slice(i * kt, (i + 1) * kt)
        st = jnp.dot(c_ref[ks, :], qlat_s[...], preferred_element_type=F32)
        b = bias_s[ks, :]
        tops = []
        for g in range(heads):
            blk = st[:, g * qb:(g + 1) * qb] + b
            s_s[ks, g * qb:(g + 1) * qb] = blk
            tops.append(jnp.max(blk.reshape(kt // SUBLANES, SUBLANES, qb), axis=0))
        m8 = jnp.maximum(m8, jnp.concatenate(tops, axis=1))
    m = jnp.max(m8, axis=0, keepdims=True)

    l8 = jnp.zeros((SUBLANES, R), F32)
    olat = jnp.zeros((dc, R), F32)
    for i in range(lk // kt):
        ks = slice(i * kt, (i + 1) * kt)
        p = jnp.exp2(s_s[ks, :] - m)
        l8 = l8 + jnp.sum(p.reshape(kt // SUBLANES, SUBLANES, R), axis=0)
        olat = olat + jnp.dot(ct_ref[:, ks], p.astype(BF16), preferred_element_type=F32)
    l = jnp.sum(l8, axis=0, keepdims=True)
    olat = (olat * (1.0 / l)).astype(BF16)
    for g in range(heads):
        ot = jnp.dot(wuvt_ref[g], olat[:, g * qb:(g + 1) * qb], preferred_element_type=F32)
        za = za_ref[:, g * dh:(g + 1) * dh]
        y_ref[:, g * dh:(g + 1) * dh] = (ot.T * (za * _sigmoid(za))).astype(y_ref.dtype)


def _attention(proj3, col_qa, col_za, mask, c3, ct3, wuk, wuvt, y_all, grp, heads_per_step=8, kt=512):
    B, S, dc = c3.shape
    n_heads, _, dh = wuk.shape
    G = min(heads_per_step, n_heads)
    gw = G * dh
    lk = (grp + 1) * Q_GROUP
    nq = Q_GROUP // Q_BLOCK
    R = G * Q_BLOCK
    kt = math.gcd(kt, lk)
    kern = functools.partial(_attn_kernel, heads=G, kt=kt)
    return pl.pallas_call(
        kern,
        grid=(B, nq, n_heads // G),
        in_specs=[pl.BlockSpec((None, Q_BLOCK, gw), lambda b, q, g: (b, grp * nq + q, col_qa // gw + g)),
                  pl.BlockSpec((None, Q_BLOCK, gw), lambda b, q, g: (b, grp * nq + q, col_za // gw + g)),
                  pl.BlockSpec((None, lk, Q_BLOCK), lambda b, q, g: (b, 0, q)),
                  pl.BlockSpec((None, lk, dc), lambda b, q, g: (b, 0, 0)),
                  pl.BlockSpec((None, dc, lk), lambda b, q, g: (b, 0, 0)),
                  pl.BlockSpec((G, dc, dh), lambda b, q, g: (g, 0, 0)),
                  pl.BlockSpec((G, dh, dc), lambda b, q, g: (g, 0, 0)),
                  pl.BlockSpec(memory_space=pl.ANY)],
        out_specs=pl.BlockSpec((None, Q_BLOCK, gw), lambda b, q, g: (b, grp * nq + q, g)),
        out_shape=jax.ShapeDtypeStruct(y_all.shape, y_all.dtype),
        input_output_aliases={7: 0},
        scratch_shapes=[pltpu.VMEM((dc, R), BF16),
                        pltpu.VMEM((lk, Q_BLOCK), F32),
                        pltpu.VMEM((lk, R), F32)],
        compiler_params=_cparams(("parallel", "parallel", "parallel")),
        name="attention",
    )(proj3, proj3, mask, c3, ct3, wuk, wuvt, y_all)


def _blockdiag_dense(w):
    nblk, blk, _ = w.shape
    per = LANES // blk
    w4 = w.reshape(nblk // per, per, blk, blk)
    eye = jnp.eye(per, dtype=w.dtype)
    dense = jnp.einsum('tgio,gh->tgiho', w4, eye)
    return dense.reshape(nblk // per, LANES, LANES).astype(BF16)


def kernel(x, g_norm, w_in, conv_w, conv_b, w_q_m, w_k_m, w_v_m, b_i, b_f, g_head_m, skip_m, g_ckv, w_uk, w_uv, g_kidx, b_kidx, w_bm, w_ba, w_out, g_final):
    B, S, D = x.shape
    depth = g_norm.shape[0]
    T = B * S
    dm = conv_w.shape[-1]
    mh = b_i.shape[-1]
    ah, dc, adh = w_uk.shape[1:]
    da = ah * adh
    di = g_kidx.shape[-1]
    n_in = w_in.shape[-1]
    n_idx = (n_in - (3 * dm + 2 * mh + da + dc + da + di + 2 * D)) // (di + 1)
    topk = min(IDX_TOPK, S // 4)
    assert 2 * mh + n_idx <= LANES and S % Q_GROUP == 0

    sizes = (dm, dm, dm, mh, mh, da, dc, da, n_idx * di, di, n_idx, D, D)
    offs = [0]
    for s in sizes:
        offs.append(offs[-1] + s)
    (o_xm, o_om, o_zm, o_ip, o_fp, o_qa, o_ckv, o_za, o_qi, o_ki, o_wi, o_gm, o_ga) = offs[:-1]

    order = [(o_xm, dm), (o_om, dm), (o_zm, dm), (o_qa, da), (o_za, da), (o_qi, n_idx * di),
             (o_gm, D), (o_ga, D), (o_ckv, dc), (o_ki, di)]
    cols, pos = {}, 0
    for off, width in order:
        cols[off] = pos
        pos += width
    c_small = pos
    pos += LANES
    tn = 1024
    n_pad = -(-pos // tn) * tn

    x2 = x.reshape(T, D)
    for l in range(depth):
        w = w_in[l]
        parts = [w[:, off:off + width].astype(BF16) for off, width in order]
        parts += [w[:, o_ip:o_ip + mh].astype(BF16), w[:, o_fp:o_fp + mh].astype(BF16),
                  w[:, o_wi:o_wi + n_idx].astype(BF16),
                  jnp.zeros((D, n_pad - c_small - 2 * mh - n_idx), BF16)]
        w_cat = jnp.concatenate(parts, axis=1)

        xn = _rmsnorm(x2, g_norm[l], BF16)
        proj = _matmul(xn, w_cat, F32, tn=tn)
        proj3 = proj.reshape(B, S, n_pad)

        gate_bias = jnp.concatenate([b_i[l], b_f[l], jnp.zeros((LANES - 2 * mh,), F32)]).reshape(1, LANES)
        y_m = _mlstm(proj3, cols[o_xm], cols[o_om], cols[o_zm], c_small, conv_w[l], conv_b[l],
                     _blockdiag_dense(w_q_m[l]), _blockdiag_dense(w_k_m[l]), _blockdiag_dense(w_v_m[l]),
                     gate_bias, g_head_m[l], skip_m[l], mh)

        c3, ct3, kidx3 = _dsa_prep(proj3, cols[o_ckv], cols[o_ki], g_ckv[l], g_kidx[l], b_kidx[l])
        wuk = w_uk[l].astype(BF16)
        wuvt = jnp.swapaxes(w_uv[l], 1, 2).astype(BF16)
        y_a = jnp.zeros((B, S, da), BF16)
        for grp in range(S // Q_GROUP):
            mask = _indexer(proj3, cols[o_qi], c_small, kidx3, grp, n_idx, 2 * mh, topk)
            y_a = _attention(proj3, cols[o_qa], cols[o_za], mask, c3, ct3, wuk, wuvt, y_a, grp)

        merged = _merge(y_m.reshape(T, dm), y_a.reshape(T, da), _to_bf16(w_bm[l]),
                        _to_bf16(w_ba[l]), proj, cols[o_gm], cols[o_ga])
        x2 = _out_proj(merged, _to_bf16(w_out[l]), x2)
    out = _rmsnorm(x2, g_final, F32)
    return out.reshape(B, S, D)
```

```python
import functools
import math

import jax
import jax.numpy as jnp
from jax import lax
from jax.experimental import pallas as pl
from jax.experimental.pallas import tpu as pltpu

EPS = 1e-6
M_QKV_BLOCK = 4
M_CONV = 4
IDX_TOPK = 256
Q_BLOCK = 128
Q_GROUP = 256
LANES = 128
SUBLANES = 8
VMEM_LIMIT = 56 * 1024 * 1024

F32 = jnp.float32
BF16 = jnp.bfloat16
INT_MIN = -2 ** 31


def _cparams(sem):
    return pltpu.CompilerParams(dimension_semantics=sem, vmem_limit_bytes=VMEM_LIMIT)


def _rmsnorm_kernel(x_ref, g_ref, o_ref):
    x = x_ref[...]
    y = x * lax.rsqrt(jnp.mean(x * x, axis=-1, keepdims=True) + EPS)
    o_ref[...] = (y * g_ref[...]).astype(o_ref.dtype)


def _rmsnorm(x2d, g, out_dtype, tm=256):
    T, D = x2d.shape
    tm = min(tm, T)
    return pl.pallas_call(
        _rmsnorm_kernel,
        grid=(T // tm,),
        in_specs=[pl.BlockSpec((tm, D), lambda i: (i, 0)),
                  pl.BlockSpec((1, D), lambda i: (0, 0))],
        out_specs=pl.BlockSpec((tm, D), lambda i: (i, 0)),
        out_shape=jax.ShapeDtypeStruct((T, D), out_dtype),
        compiler_params=_cparams(("parallel",)),
        name="rmsnorm",
    )(x2d, g.reshape(1, D))


def _cast_kernel(x_ref, o_ref):
    o_ref[...] = x_ref[...].astype(o_ref.dtype)


def _to_bf16(w, tm=512):
    M, N = w.shape
    tm = min(tm, M)
    return pl.pallas_call(
        _cast_kernel,
        grid=(M // tm,),
        in_specs=[pl.BlockSpec((tm, N), lambda i: (i, 0))],
        out_specs=pl.BlockSpec((tm, N), lambda i: (i, 0)),
        out_shape=jax.ShapeDtypeStruct((M, N), BF16),
        compiler_params=_cparams(("parallel",)),
        name="cast_bf16",
    )(w)


def _relayout_kernel(*refs, shift):
    if shift:
        a_ref, b_ref, _, o_ref = refs
        x = jnp.concatenate([a_ref[:, shift:], b_ref[:, :shift]], axis=1)
    else:
        a_ref, _, o_ref = refs
        x = a_ref[...]
    o_ref[...] = x.astype(o_ref.dtype)


def _relayout(src, layer, src_off, width, w_cat, dst_col, max_bw=512):
    D = src.shape[1]
    shift = src_off % LANES
    start = src_off - shift
    bw = math.gcd(math.gcd(start, dst_col), math.gcd(width, max_bw))
    assert bw % LANES == 0 and dst_col % bw == 0 and width % bw == 0
    lanes_per_blk = bw // LANES
    in_specs = [pl.BlockSpec((None, D, bw), lambda j: (layer, 0, start // bw + j))]
    operands = [src]
    if shift:
        in_specs.append(pl.BlockSpec(
            (None, D, LANES), lambda j: (layer, 0, start // LANES + (j + 1) * lanes_per_blk)))
        operands.append(src)
    in_specs.append(pl.BlockSpec(memory_space=pl.ANY))
    operands.append(w_cat)
    return pl.pallas_call(
        functools.partial(_relayout_kernel, shift=shift),
        grid=(width // bw,),
        in_specs=in_specs,
        out_specs=pl.BlockSpec((D, bw), lambda j: (0, dst_col // bw + j)),
        out_shape=jax.ShapeDtypeStruct(w_cat.shape, w_cat.dtype),
        input_output_aliases={len(operands) - 1: 0},
        compiler_params=_cparams(("parallel",)),
        name="w_relayout",
    )(*operands)


def _mm_kernel(a_ref, w_ref, o_ref):
    o_ref[...] = jnp.dot(a_ref[...], w_ref[...], preferred_element_type=F32).astype(o_ref.dtype)


def _matmul(a, w, out_dtype, tm=1024, tn=512):
    M, K = a.shape
    N = w.shape[1]
    tm, tn = min(tm, M), min(tn, N)
    return pl.pallas_call(
        _mm_kernel,
        grid=(M // tm, N // tn),
        in_specs=[pl.BlockSpec((tm, K), lambda i, j: (i, 0)),
                  pl.BlockSpec((K, tn), lambda i, j: (0, j))],
        out_specs=pl.BlockSpec((tm, tn), lambda i, j: (i, j)),
        out_shape=jax.ShapeDtypeStruct((M, N), out_dtype),
        compiler_params=_cparams(("parallel", "parallel")),
        name="in_proj",
    )(a, w)


def _merge_kernel(ym_ref, ya_ref, wbm_ref, wba_ref, gm_ref, ga_ref, o_ref):
    half = o_ref.shape[1] // 2
    for c in range(2):
        cs = slice(c * half, (c + 1) * half)
        pm = jnp.dot(ym_ref[...], wbm_ref[:, cs], preferred_element_type=F32)
        pa = jnp.dot(ya_ref[...], wba_ref[:, cs], preferred_element_type=F32)
        o = _sigmoid(gm_ref[:, cs]) * pm + _sigmoid(ga_ref[:, cs]) * pa
        o_ref[:, cs] = o.astype(o_ref.dtype)


def _merge(ym, ya, wbm, wba, proj, gm_col, ga_col, tm=512, tn=512):
    M, K = ym.shape
    N = wbm.shape[1]
    tm, tn = min(tm, M), min(tn, N)
    gm_blk, ga_blk = gm_col // tn, ga_col // tn
    return pl.pallas_call(
        _merge_kernel,
        grid=(M // tm, N // tn),
        in_specs=[pl.BlockSpec((tm, K), lambda i, j: (i, 0)),
                  pl.BlockSpec((tm, K), lambda i, j: (i, 0)),
                  pl.BlockSpec((K, tn), lambda i, j: (0, j)),
                  pl.BlockSpec((K, tn), lambda i, j: (0, j)),
                  pl.BlockSpec((tm, tn), lambda i, j: (i, gm_blk + j)),
                  pl.BlockSpec((tm, tn), lambda i, j: (i, ga_blk + j))],
        out_specs=pl.BlockSpec((tm, tn), lambda i, j: (i, j)),
        out_shape=jax.ShapeDtypeStruct((M, N), BF16),
        compiler_params=_cparams(("parallel", "parallel")),
        name="merge",
    )(ym, ya, wbm, wba, proj, proj)


def _out_kernel(a_ref, w_ref, x_ref, o_ref):
    o_ref[...] = x_ref[...] + jnp.dot(a_ref[...], w_ref[...], preferred_element_type=F32)


def _out_proj(a, w, x2d, tm=1024, tn=512):
    M, K = a.shape
    N = w.shape[1]
    tm, tn = min(tm, M), min(tn, N)
    return pl.pallas_call(
        _out_kernel,
        grid=(M // tm, N // tn),
        in_specs=[pl.BlockSpec((tm, K), lambda i, j: (i, 0)),
                  pl.BlockSpec((K, tn), lambda i, j: (0, j)),
                  pl.BlockSpec((tm, tn), lambda i, j: (i, j))],
        out_specs=pl.BlockSpec((tm, tn), lambda i, j: (i, j)),
        out_shape=jax.ShapeDtypeStruct((M, N), F32),
        compiler_params=_cparams(("parallel", "parallel")),
        name="out_proj",
    )(a, w, x2d)


def _sigmoid(x):
    return 0.5 * jnp.tanh(0.5 * x) + 0.5


def _log_sigmoid(x):
    return -(jnp.maximum(-x, 0.0) + jnp.log1p(jnp.exp(-jnp.abs(x))))


def _mlstm_kernel(xm_ref, om_ref, zm_ref, sm_ref, cw_ref, cb_ref, wq_ref, wk_ref, wv_ref,
                  gb_ref, gh_ref, sk_ref, y_ref,
                  xbuf, xc_s, q_s, k_s, v_s, c_s, n_s, m_s, *, n_heads, rows, chunk):
    h = pl.program_id(1)
    j = pl.program_id(2)
    dh = xc_s.shape[1]
    L = chunk
    pad = SUBLANES

    @pl.when(j == 0)
    def _():
        xbuf[0:pad, :] = jnp.zeros((pad, dh), F32)
        c_s[...] = jnp.zeros_like(c_s)
        n_s[...] = jnp.zeros_like(n_s)
        m_s[...] = jnp.zeros_like(m_s)

    xblk = xm_ref[...]
    xbuf[pad:pad + rows, :] = xblk
    acc = jnp.zeros((rows, dh), F32) + cb_ref[...]
    for k in range(M_CONV):
        acc = acc + cw_ref[k:k + 1, :] * xbuf[pl.ds(pad - (M_CONV - 1) + k, rows), :]
    xc = acc * _sigmoid(acc)
    xc_s[...] = xc
    xbuf[0:pad, :] = xblk[rows - pad:rows, :]

    scale = dh ** -0.5
    for t in range(dh // LANES):
        sl = slice(t * LANES, (t + 1) * LANES)
        xct = xc[:, sl].astype(BF16)
        xmt = xblk[:, sl].astype(BF16)
        q_s[:, sl] = jnp.dot(xct, wq_ref[t], preferred_element_type=F32).astype(BF16)
        k_s[:, sl] = jnp.dot(xct, wk_ref[t], preferred_element_type=F32) * scale
        v_s[:, sl] = jnp.dot(xmt, wv_ref[t], preferred_element_type=F32).astype(BF16)

    lane = lax.broadcasted_iota(jnp.int32, (L, LANES), 1)
    row_i = lax.broadcasted_iota(jnp.int32, (L, L), 0)
    col_i = lax.broadcasted_iota(jnp.int32, (L, L), 1)
    tril = col_i <= row_i
    eye = col_i == row_i

    def to_row(col):
        return jnp.sum(jnp.where(eye, col, 0.0), axis=0, keepdims=True)

    def chunk(c, carry):
        r0 = pl.multiple_of(c * L, L)
        rs = pl.ds(r0, L)
        g = sm_ref[rs, :] + gb_ref[...]
        i_col = jnp.sum(jnp.where(lane == h, g, 0.0), axis=1, keepdims=True)
        f_col = _log_sigmoid(jnp.sum(jnp.where(lane == n_heads + h, g, 0.0), axis=1, keepdims=True))
        i_row = to_row(i_col)
        f_row = to_row(f_col)
        b_col = jnp.sum(jnp.where(tril, f_row, 0.0), axis=1, keepdims=True)
        b_row = jnp.sum(jnp.where(col_i >= row_i, f_col, 0.0), axis=0, keepdims=True)
        m_prev = m_s[...]
        dmat = jnp.where(tril, b_col - b_row + i_row, -jnp.inf)
        m_inter = b_col + m_prev
        m_t = jnp.maximum(m_inter, jnp.max(dmat, axis=1, keepdims=True))
        qc = q_s[rs, :]
        kc = k_s[rs, :]
        vc = v_s[rs, :]
        s_qk = lax.dot_general(qc, kc.astype(BF16), (((1,), (1,)), ((), ())),
                               preferred_element_type=F32)
        a = s_qk * jnp.exp(dmat - m_t)
        inter = jnp.exp(m_inter - m_t)
        c_prev = c_s[...]
        num = (jnp.dot(a.astype(BF16), vc, preferred_element_type=F32)
               + inter * jnp.dot(qc, c_prev.astype(BF16), preferred_element_type=F32))
        qn = jnp.sum(qc.astype(F32) * n_s[...], axis=1, keepdims=True)
        den = jnp.sum(a, axis=1, keepdims=True) + inter * qn
        hh = num / jnp.maximum(jnp.abs(den), jnp.exp(-m_t))

        mu = jnp.mean(hh, axis=1, keepdims=True)
        hc = hh - mu
        hn = hc * lax.rsqrt(jnp.mean(hc * hc, axis=1, keepdims=True) + EPS) * gh_ref[...]
        hn = _sigmoid(om_ref[rs, :]) * hn
        zm = zm_ref[rs, :]
        y = (hn + sk_ref[...] * xc_s[rs, :]) * (zm * _sigmoid(zm))
        y_ref[rs, :] = y.astype(y_ref.dtype)

        b_last = b_col[L - 1:L, :]
        g_col = b_last - b_col + i_col
        m_new = jnp.maximum(b_last + m_prev, jnp.max(g_col, axis=0, keepdims=True))
        w_col = jnp.exp(g_col - m_new)
        decay = jnp.exp(b_last + m_prev - m_new)
        kw = kc * w_col
        c_s[...] = decay * c_prev + lax.dot_general(
            kw.astype(BF16), vc, (((0,), (0,)), ((), ())), preferred_element_type=F32)
        n_s[...] = decay * n_s[...] + jnp.sum(kw, axis=0, keepdims=True)
        m_s[...] = m_new
        return carry

    lax.fori_loop(0, rows // L, chunk, 0, unroll=True)


def _mlstm(proj3, col_xm, col_om, col_zm, col_small, conv_w, conv_b, wq_d, wk_d, wv_d,
           gate_bias, g_head, skip, n_heads, rows=256, chunk=256):
    B, S, _ = proj3.shape
    dm = conv_w.shape[1]
    dh = dm // n_heads
    rows = min(rows, S)
    chunk = min(chunk, rows)
    assert rows % chunk == 0 and S % rows == 0
    nt = dh // LANES
    bx, bo, bz, bs = col_xm // dh, col_om // dh, col_zm // dh, col_small // LANES
    seq = lambda base: pl.BlockSpec((None, rows, dh), lambda b, h, j: (b, j, base + h))
    vec = pl.BlockSpec((1, dh), lambda b, h, j: (0, h))
    wsp = pl.BlockSpec((nt, LANES, LANES), lambda b, h, j: (h, 0, 0))
    kern = functools.partial(_mlstm_kernel, n_heads=n_heads, rows=rows, chunk=chunk)
    return pl.pallas_call(
        kern,
        grid=(B, n_heads, S // rows),
        in_specs=[seq(bx), seq(bo), seq(bz),
                  pl.BlockSpec((None, rows, LANES), lambda b, h, j: (b, j, bs)),
                  pl.BlockSpec((M_CONV, dh), lambda b, h, j: (0, h)),
                  vec, wsp, wsp, wsp,
                  pl.BlockSpec((1, LANES), lambda b, h, j: (0, 0)),
                  vec, vec],
        out_specs=pl.BlockSpec((None, rows, dh), lambda b, h, j: (b, j, h)),
        out_shape=jax.ShapeDtypeStruct((B, S, dm), BF16),
        scratch_shapes=[pltpu.VMEM((rows + SUBLANES, dh), F32),
                        pltpu.VMEM((rows, dh), F32),
                        pltpu.VMEM((rows, dh), BF16),
                        pltpu.VMEM((rows, dh), F32),
                        pltpu.VMEM((rows, dh), BF16),
                        pltpu.VMEM((dh, dh), F32),
                        pltpu.VMEM((1, dh), F32),
                        pltpu.VMEM((1, 1), F32)],
        compiler_params=_cparams(("parallel", "parallel", "arbitrary")),
        name="mlstm",
    )(proj3, proj3, proj3, proj3, conv_w, conv_b.reshape(1, dm), wq_d, wk_d, wv_d,
      gate_bias, g_head.reshape(1, dm), skip.reshape(1, dm))


def _dsa_prep_kernel(ckv_ref, ki_ref, gc_ref, gk_ref, bk_ref, c_ref, ct_ref, kidx_ref):
    x = ckv_ref[...]
    c = x * lax.rsqrt(jnp.mean(x * x, axis=-1, keepdims=True) + EPS) * gc_ref[...]
    c_ref[...] = c.astype(c_ref.dtype)
    ct_ref[...] = c.T.astype(ct_ref.dtype)
    k = ki_ref[...]
    mu = jnp.mean(k, axis=-1, keepdims=True)
    kc = k - mu
    y = kc * lax.rsqrt(jnp.mean(kc * kc, axis=-1, keepdims=True) + EPS)
    kidx_ref[...] = (y * gk_ref[...] + bk_ref[...]).astype(kidx_ref.dtype)


def _dsa_prep(proj3, col_ckv, col_ki, g_ckv, g_kidx, b_kidx, tm=512):
    B, S, _ = proj3.shape
    dc, di = g_ckv.shape[0], g_kidx.shape[0]
    tm = min(tm, S)
    return pl.pallas_call(
        _dsa_prep_kernel,
        grid=(B, S // tm),
        in_specs=[pl.BlockSpec((None, tm, dc), lambda b, i: (b, i, col_ckv // dc)),
                  pl.BlockSpec((None, tm, di), lambda b, i: (b, i, col_ki // di)),
                  pl.BlockSpec((1, dc), lambda b, i: (0, 0)),
                  pl.BlockSpec((1, di), lambda b, i: (0, 0)),
                  pl.BlockSpec((1, di), lambda b, i: (0, 0))],
        out_specs=[pl.BlockSpec((None, tm, dc), lambda b, i: (b, i, 0)),
                   pl.BlockSpec((None, dc, tm), lambda b, i: (b, 0, i)),
                   pl.BlockSpec((None, tm, di), lambda b, i: (b, i, 0))],
        out_shape=[jax.ShapeDtypeStruct((B, S, dc), BF16),
                   jax.ShapeDtypeStruct((B, dc, S), BF16),
                   jax.ShapeDtypeStruct((B, S, di), BF16)],
        compiler_params=_cparams(("parallel", "parallel")),
        name="dsa_prep",
    )(proj3, proj3, g_ckv.reshape(1, dc), g_kidx.reshape(1, di), b_kidx.reshape(1, di))


def _indexer_kernel(qi_ref, sm_ref, kidx_ref, mask_ref, score_s, key_s, wt_s,
                    *, n_idx, wi_lane, topk, q0):
    lk, di = kidx_ref.shape
    qg = qi_ref.shape[0]
    kf = float(topk)
    wt_s[...] = sm_ref[...].T * ((n_idx ** -0.5) * (di ** -0.5))
    kidx = kidx_ref[...]
    score_s[...] = jnp.zeros_like(score_s)

    def head(hh, carry):
        c0 = pl.multiple_of(hh * di, di)
        qh = qi_ref[:, pl.ds(c0, di)].astype(BF16)
        logits = lax.dot_general(kidx, qh, (((1,), (1,)), ((), ())), preferred_element_type=F32)
        w = wt_s[pl.ds(wi_lane + hh, 1), :]
        score_s[...] += jnp.maximum(logits, 0.0) * w
        return carry

    lax.fori_loop(0, n_idx, head, 0, unroll=2)

    bits = pltpu.bitcast(score_s[...], jnp.int32)
    key = jnp.where(bits >= 0, bits, bits ^ jnp.int32(0x7FFFFFFF))
    s_pos = lax.broadcasted_iota(jnp.int32, (lk, qg), 0)
    t_pos = q0 + lax.broadcasted_iota(jnp.int32, (lk, qg), 1)
    causal = s_pos <= t_pos
    key_s[...] = jnp.where(causal, key, jnp.int32(INT_MIN))

    def count(pred):
        return jnp.sum(jnp.where(pred, 1.0, 0.0), axis=0, keepdims=True)

    thr = jnp.where(count(key_s[...] >= 0) >= kf, jnp.int32(0), jnp.int32(INT_MIN))

    def bit_step(it, thr):
        cand = thr + jnp.left_shift(jnp.int32(1), 30 - it)
        return jnp.where(count(key_s[...] >= cand) >= kf, cand, thr)

    thr = lax.fori_loop(0, 31, bit_step, thr)
    key = key_s[...]
    ge = key >= thr
    mask_ref[...] = jnp.where(jnp.logical_and(ge, causal), 1, 0).astype(mask_ref.dtype)

    excess = jnp.logical_and(count(ge) > kf, thr > jnp.int32(INT_MIN))

    @pl.when(jnp.max(jnp.where(excess, 1.0, 0.0)) > 0.0)
    def _():
        gt = key > thr
        tie = key == thr
        need = kf - count(gt)
        nbits = max(1, (lk - 1).bit_length())

        def idx_step(it, x):
            cand = x + jnp.left_shift(jnp.int32(1), nbits - 1 - it)
            c = count(jnp.logical_and(tie, s_pos < cand))
            return jnp.where(c < need, cand, x)

        x = lax.fori_loop(0, nbits, idx_step, jnp.zeros((1, qg), jnp.int32))
        sel = jnp.logical_or(gt, jnp.logical_and(tie, s_pos <= x))
        mask_ref[...] = jnp.where(jnp.logical_and(sel, causal), 1, 0).astype(mask_ref.dtype)


def _indexer(proj3, col_qi, col_small, kidx3, grp, n_idx, wi_lane, topk):
    B, S, di = kidx3.shape
    wq = n_idx * di
    lk = (grp + 1) * Q_GROUP
    kern = functools.partial(_indexer_kernel, n_idx=n_idx, wi_lane=wi_lane, topk=topk,
                             q0=grp * Q_GROUP)
    return pl.pallas_call(
        kern,
        grid=(B,),
        in_specs=[pl.BlockSpec((None, Q_GROUP, wq), lambda b: (b, grp, col_qi // wq)),
                  pl.BlockSpec((None, Q_GROUP, LANES), lambda b: (b, grp, col_small // LANES)),
                  pl.BlockSpec((None, lk, di), lambda b: (b, 0, 0))],
        out_specs=pl.BlockSpec((None, lk, Q_GROUP), lambda b: (b, 0, 0)),
        out_shape=jax.ShapeDtypeStruct((B, lk, Q_GROUP), jnp.int8),
        scratch_shapes=[pltpu.VMEM((lk, Q_GROUP), F32),
                        pltpu.VMEM((lk, Q_GROUP), jnp.int32),
                        pltpu.VMEM((LANES, Q_GROUP), F32)],
        compiler_params=_cparams(("parallel",)),
        name="indexer",
    )(proj3, proj3, kidx3)


def _attn_kernel(qa_ref, za_ref, mask_ref, c_ref, ct_ref, wuk_ref, wuvt_ref, y_all_ref, y_ref,
                 qlat_s, bias_s, s_s, *, heads, kt):
    del y_all_ref
    lk, dc = c_ref.shape
    dh = wuk_ref.shape[2]
    qb = qa_ref.shape[0]
    R = heads * qb
    assert lk % kt == 0 and kt % SUBLANES == 0
    scale = (dh ** -0.5) * math.log2(math.e)
    nt = (((1,), (1,)), ((), ()))

    for g in range(heads):
        qh = qa_ref[:, g * dh:(g + 1) * dh].astype(BF16)
        ql = lax.dot_general(wuk_ref[g], qh, nt, preferred_element_type=F32)
        qlat_s[:, g * qb:(g + 1) * qb] = (ql * scale).astype(BF16)
    bias_s[...] = jnp.where(mask_ref[...].astype(jnp.int32) != 0, 0.0, -jnp.inf)

    m8 = jnp.full((SUBLANES, R), -jnp.inf, F32)
    for i in range(lk // kt):
        ks = slice(i * kt, (i + 1) * kt)
        st = jnp.dot(c_ref[ks, :], qlat_s[...], preferred_element_type=F32)
        b = bias_s[ks, :]
        tops = []
        for g in range(heads):
            blk = st[:, g * qb:(g + 1) * qb] + b
            s_s[ks, g * qb:(g + 1) * qb] = blk
            tops.append(jnp.max(blk.reshape(kt // SUBLANES, SUBLANES, qb), axis=0))
        m8 = jnp.maximum(m8, jnp.concatenate(tops, axis=1))
    m = jnp.max(m8, axis=0, keepdims=True)

    l8 = jnp.zeros((SUBLANES, R), F32)
    olat = jnp.zeros((dc, R), F32)
    for i in range(lk // kt):
        ks = slice(i * kt, (i + 1) * kt)
        p = jnp.exp2(s_s[ks, :] - m)
        l8 = l8 + jnp.sum(p.reshape(kt // SUBLANES, SUBLANES, R), axis=0)
        olat = olat + jnp.dot(ct_ref[:, ks], p.astype(BF16), preferred_element_type=F32)
    l = jnp.sum(l8, axis=0, keepdims=True)
    olat = (olat * (1.0 / l)).astype(BF16)
    for g in range(heads):
        ot = jnp.dot(wuvt_ref[g], olat[:, g * qb:(g + 1) * qb], preferred_element_type=F32)
        za = za_ref[:, g * dh:(g + 1) * dh]
        y_ref[:, g * dh:(g + 1) * dh] = (ot.T * (za * _sigmoid(za))).astype(y_ref.dtype)


def _attention(proj3, col_qa, col_za, mask, c3, ct3, wuk, wuvt, y_all, grp, heads_per_step=16, kt=512):
    B, S, dc = c3.shape
    n_heads, _, dh = wuk.shape
    G = min(heads_per_step, n_heads)
    gw = G * dh
    lk = (grp + 1) * Q_GROUP
    nq = Q_GROUP // Q_BLOCK
    R = G * Q_BLOCK
    kt = math.gcd(kt, lk)
    kern = functools.partial(_attn_kernel, heads=G, kt=kt)
    return pl.pallas_call(
        kern,
        grid=(B, nq, n_heads // G),
        in_specs=[pl.BlockSpec((None, Q_BLOCK, gw), lambda b, q, g: (b, grp * nq + q, col_qa // gw + g)),
                  pl.BlockSpec((None, Q_BLOCK, gw), lambda b, q, g: (b, grp * nq + q, col_za // gw + g)),
                  pl.BlockSpec((None, lk, Q_BLOCK), lambda b, q, g: (b, 0, q)),
                  pl.BlockSpec((None, lk, dc), lambda b, q, g: (b, 0, 0)),
                  pl.BlockSpec((None, dc, lk), lambda b, q, g: (b, 0, 0)),
                  pl.BlockSpec((G, dc, dh), lambda b, q, g: (g, 0, 0)),
                  pl.BlockSpec((G, dh, dc), lambda b, q, g: (g, 0, 0)),
                  pl.BlockSpec(memory_space=pl.ANY)],
        out_specs=pl.BlockSpec((None, Q_BLOCK, gw), lambda b, q, g: (b, grp * nq + q, g)),
        out_shape=jax.ShapeDtypeStruct(y_all.shape, y_all.dtype),
        input_output_aliases={7: 0},
        scratch_shapes=[pltpu.VMEM((dc, R), BF16),
                        pltpu.VMEM((lk, Q_BLOCK), F32),
                        pltpu.VMEM((lk, R), F32)],
        compiler_params=_cparams(("parallel", "parallel", "parallel")),
        name="attention",
    )(proj3, proj3, mask, c3, ct3, wuk, wuvt, y_all)


def _blockdiag_dense(w):
    nblk, blk, _ = w.shape
    per = LANES // blk
    w4 = w.reshape(nblk // per, per, blk, blk)
    eye = jnp.eye(per, dtype=w.dtype)
    dense = jnp.einsum('tgio,gh->tgiho', w4, eye)
    return dense.reshape(nblk // per, LANES, LANES).astype(BF16)


def kernel(x, g_norm, w_in, conv_w, conv_b, w_q_m, w_k_m, w_v_m, b_i, b_f, g_head_m, skip_m, g_ckv, w_uk, w_uv, g_kidx, b_kidx, w_bm, w_ba, w_out, g_final):
    B, S, D = x.shape
    depth = g_norm.shape[0]
    T = B * S
    dm = conv_w.shape[-1]
    mh = b_i.shape[-1]
    ah, dc, adh = w_uk.shape[1:]
    da = ah * adh
    di = g_kidx.shape[-1]
    n_in = w_in.shape[-1]
    n_idx = (n_in - (3 * dm + 2 * mh + da + dc + da + di + 2 * D)) // (di + 1)
    topk = min(IDX_TOPK, S // 4)
    assert 2 * mh + n_idx <= LANES and S % Q_GROUP == 0

    sizes = (dm, dm, dm, mh, mh, da, dc, da, n_idx * di, di, n_idx, D, D)
    offs = [0]
    for s in sizes:
        offs.append(offs[-1] + s)
    (o_xm, o_om, o_zm, o_ip, o_fp, o_qa, o_ckv, o_za, o_qi, o_ki, o_wi, o_gm, o_ga) = offs[:-1]

    order = [(o_xm, dm), (o_om, dm), (o_zm, dm), (o_qa, da), (o_za, da), (o_qi, n_idx * di),
             (o_gm, D), (o_ga, D), (o_ckv, dc), (o_ki, di)]
    cols, pos = {}, 0
    for off, width in order:
        cols[off] = pos
        pos += width
    c_small = pos
    pos += LANES
    tn = 1024
    n_pad = -(-pos // tn) * tn

    runs = []
    for off, width in order:
        if runs and runs[-1][0] + runs[-1][1] == off and runs[-1][2] + runs[-1][1] == cols[off]:
            runs[-1] = (runs[-1][0], runs[-1][1] + width, runs[-1][2])
        else:
            runs.append((off, width, cols[off]))

    x2 = x.reshape(T, D)
    for l in range(depth):
        w = w_in[l]
        w_cat = jnp.zeros((D, n_pad), BF16)
        for off, width, dst in runs:
            w_cat = _relayout(w_in, l, off, width, w_cat, dst)
        small = jnp.concatenate([w[:, o_ip:o_ip + mh], w[:, o_fp:o_fp + mh], w[:, o_wi:o_wi + n_idx],
                                 jnp.zeros((D, LANES - 2 * mh - n_idx), F32)], axis=1)
        w_cat = _relayout(small[None], 0, 0, LANES, w_cat, c_small)

        xn = _rmsnorm(x2, g_norm[l], BF16)
        proj = _matmul(xn, w_cat, F32, tn=tn)
        proj3 = proj.reshape(B, S, n_pad)

        gate_bias = jnp.concatenate([b_i[l], b_f[l], jnp.zeros((LANES - 2 * mh,), F32)]).reshape(1, LANES)
        y_m = _mlstm(proj3, cols[o_xm], cols[o_om], cols[o_zm], c_small, conv_w[l], conv_b[l],
                     _blockdiag_dense(w_q_m[l]), _blockdiag_dense(w_k_m[l]), _blockdiag_dense(w_v_m[l]),
                     gate_bias, g_head_m[l], skip_m[l], mh)

        c3, ct3, kidx3 = _dsa_prep(proj3, cols[o_ckv], cols[o_ki], g_ckv[l], g_kidx[l], b_kidx[l])
        wuk = w_uk[l].astype(BF16)
        wuvt = jnp.swapaxes(w_uv[l], 1, 2).astype(BF16)
        y_a = jnp.zeros((B, S, da), BF16)
        for grp in range(S // Q_GROUP):
            mask = _indexer(proj3, cols[o_qi], c_small, kidx3, grp, n_idx, 2 * mh, topk)
            y_a = _attention(proj3, cols[o_qa], cols[o_za], mask, c3, ct3, wuk, wuvt, y_a, grp)

        merged = _merge(y_m.reshape(T, dm), y_a.reshape(T, da), _to_bf16(w_bm[l]),
                        _to_bf16(w_ba[l]), proj, cols[o_gm], cols[o_ga])
        x2 = _out_proj(merged, _to_bf16(w_out[l]), x2)
    out = _rmsnorm(x2, g_final, F32)
    return out.reshape(B, S, D)
```

```python
import functools
import math

import jax
import jax.numpy as jnp
from jax import lax
from jax.experimental import pallas as pl
from jax.experimental.pallas import tpu as pltpu

EPS = 1e-6
M_QKV_BLOCK = 4
M_CONV = 4
IDX_TOPK = 256
Q_BLOCK = 128
Q_GROUP = 256
LANES = 128
SUBLANES = 8
VMEM_LIMIT = 56 * 1024 * 1024

F32 = jnp.float32
BF16 = jnp.bfloat16
INT_MIN = -2 ** 31


def _cparams(sem):
    return pltpu.CompilerParams(dimension_semantics=sem, vmem_limit_bytes=VMEM_LIMIT)


def _rmsnorm_kernel(x_ref, g_ref, o_ref):
    x = x_ref[...]
    y = x * lax.rsqrt(jnp.mean(x * x, axis=-1, keepdims=True) + EPS)
    o_ref[...] = (y * g_ref[...]).astype(o_ref.dtype)


def _rmsnorm(x2d, g, out_dtype, tm=256):
    T, D = x2d.shape
    tm = min(tm, T)
    return pl.pallas_call(
        _rmsnorm_kernel,
        grid=(T // tm,),
        in_specs=[pl.BlockSpec((tm, D), lambda i: (i, 0)),
                  pl.BlockSpec((1, D), lambda i: (0, 0))],
        out_specs=pl.BlockSpec((tm, D), lambda i: (i, 0)),
        out_shape=jax.ShapeDtypeStruct((T, D), out_dtype),
        compiler_params=_cparams(("parallel",)),
        name="rmsnorm",
    )(x2d, g.reshape(1, D))


def _cast_kernel(x_ref, o_ref):
    o_ref[...] = x_ref[...].astype(o_ref.dtype)


def _to_bf16(w_stack, layer, rows, tm=512):
    N = w_stack.shape[1]
    tm = min(tm, rows)
    nb = rows // tm
    return pl.pallas_call(
        _cast_kernel,
        grid=(nb,),
        in_specs=[pl.BlockSpec((tm, N), lambda i: (layer * nb + i, 0))],
        out_specs=pl.BlockSpec((tm, N), lambda i: (i, 0)),
        out_shape=jax.ShapeDtypeStruct((rows, N), BF16),
        compiler_params=_cparams(("parallel",)),
        name="cast_bf16",
    )(w_stack)


def _relayout_kernel(*refs, shift):
    if shift:
        a_ref, b_ref, _, o_ref = refs
        x = jnp.concatenate([a_ref[:, shift:], b_ref[:, :shift]], axis=1)
    else:
        a_ref, _, o_ref = refs
        x = a_ref[...]
    o_ref[...] = x.astype(o_ref.dtype)


def _relayout(src, layer, src_off, width, w_cat, dst_col, max_bw=512):
    D = w_cat.shape[0]
    shift = src_off % LANES
    start = src_off - shift
    bw = math.gcd(math.gcd(start, dst_col), math.gcd(width, max_bw))
    assert bw % LANES == 0 and dst_col % bw == 0 and width % bw == 0
    lanes_per_blk = bw // LANES
    in_specs = [pl.BlockSpec((D, bw), lambda j: (layer, start // bw + j))]
    operands = [src]
    if shift:
        in_specs.append(pl.BlockSpec(
            (D, LANES), lambda j: (layer, start // LANES + (j + 1) * lanes_per_blk)))
        operands.append(src)
    in_specs.append(pl.BlockSpec(memory_space=pl.ANY))
    operands.append(w_cat)
    return pl.pallas_call(
        functools.partial(_relayout_kernel, shift=shift),
        grid=(width // bw,),
        in_specs=in_specs,
        out_specs=pl.BlockSpec((D, bw), lambda j: (0, dst_col // bw + j)),
        out_shape=jax.ShapeDtypeStruct(w_cat.shape, w_cat.dtype),
        input_output_aliases={len(operands) - 1: 0},
        compiler_params=_cparams(("parallel",)),
        name="w_relayout",
    )(*operands)


def _mm_kernel(a_ref, w_ref, o_ref):
    o_ref[...] = jnp.dot(a_ref[...], w_ref[...], preferred_element_type=F32).astype(o_ref.dtype)


def _matmul(a, w, out_dtype, tm=1024, tn=512):
    M, K = a.shape
    N = w.shape[1]
    tm, tn = min(tm, M), min(tn, N)
    return pl.pallas_call(
        _mm_kernel,
        grid=(M // tm, N // tn),
        in_specs=[pl.BlockSpec((tm, K), lambda i, j: (i, 0)),
                  pl.BlockSpec((K, tn), lambda i, j: (0, j))],
        out_specs=pl.BlockSpec((tm, tn), lambda i, j: (i, j)),
        out_shape=jax.ShapeDtypeStruct((M, N), out_dtype),
        compiler_params=_cparams(("parallel", "parallel")),
        name="in_proj",
    )(a, w)


def _merge_kernel(ym_ref, ya_ref, wbm_ref, wba_ref, gm_ref, ga_ref, o_ref):
    half = o_ref.shape[1] // 2
    for c in range(2):
        cs = slice(c * half, (c + 1) * half)
        pm = jnp.dot(ym_ref[...], wbm_ref[:, cs], preferred_element_type=F32)
        pa = jnp.dot(ya_ref[...], wba_ref[:, cs], preferred_element_type=F32)
        o = _sigmoid(gm_ref[:, cs]) * pm + _sigmoid(ga_ref[:, cs]) * pa
        o_ref[:, cs] = o.astype(o_ref.dtype)


def _merge(ym, ya, wbm, wba, proj, gm_col, ga_col, tm=512, tn=512):
    M, K = ym.shape
    N = wbm.shape[1]
    tm, tn = min(tm, M), min(tn, N)
    gm_blk, ga_blk = gm_col // tn, ga_col // tn
    return pl.pallas_call(
        _merge_kernel,
        grid=(M // tm, N // tn),
        in_specs=[pl.BlockSpec((tm, K), lambda i, j: (i, 0)),
                  pl.BlockSpec((tm, K), lambda i, j: (i, 0)),
                  pl.BlockSpec((K, tn), lambda i, j: (0, j)),
                  pl.BlockSpec((K, tn), lambda i, j: (0, j)),
                  pl.BlockSpec((tm, tn), lambda i, j: (i, gm_blk + j)),
                  pl.BlockSpec((tm, tn), lambda i, j: (i, ga_blk + j))],
        out_specs=pl.BlockSpec((tm, tn), lambda i, j: (i, j)),
        out_shape=jax.ShapeDtypeStruct((M, N), BF16),
        compiler_params=_cparams(("parallel", "parallel")),
        name="merge",
    )(ym, ya, wbm, wba, proj, proj)


def _out_kernel(a_ref, w_ref, x_ref, o_ref):
    o_ref[...] = x_ref[...] + jnp.dot(a_ref[...], w_ref[...], preferred_element_type=F32)


def _out_proj(a, w, x2d, tm=1024, tn=512):
    M, K = a.shape
    N = w.shape[1]
    tm, tn = min(tm, M), min(tn, N)
    return pl.pallas_call(
        _out_kernel,
        grid=(M // tm, N // tn),
        in_specs=[pl.BlockSpec((tm, K), lambda i, j: (i, 0)),
                  pl.BlockSpec((K, tn), lambda i, j: (0, j)),
                  pl.BlockSpec((tm, tn), lambda i, j: (i, j))],
        out_specs=pl.BlockSpec((tm, tn), lambda i, j: (i, j)),
        out_shape=jax.ShapeDtypeStruct((M, N), F32),
        compiler_params=_cparams(("parallel", "parallel")),
        name="out_proj",
    )(a, w, x2d)


def _sigmoid(x):
    return 0.5 * jnp.tanh(0.5 * x) + 0.5


def _log_sigmoid(x):
    return -(jnp.maximum(-x, 0.0) + jnp.log1p(jnp.exp(-jnp.abs(x))))


def _mlstm_kernel(xm_ref, om_ref, zm_ref, sm_ref, cw_ref, cb_ref, wq_ref, wk_ref, wv_ref,
                  gb_ref, gh_ref, sk_ref, y_ref,
                  xbuf, xc_s, q_s, k_s, v_s, c_s, n_s, m_s, *, n_heads, rows, chunk):
    h = pl.program_id(1)
    j = pl.program_id(2)
    dh = xc_s.shape[1]
    L = chunk
    pad = SUBLANES

    @pl.when(j == 0)
    def _():
        xbuf[0:pad, :] = jnp.zeros((pad, dh), F32)
        c_s[...] = jnp.zeros_like(c_s)
        n_s[...] = jnp.zeros_like(n_s)
        m_s[...] = jnp.zeros_like(m_s)

    xblk = xm_ref[...]
    xbuf[pad:pad + rows, :] = xblk
    acc = jnp.zeros((rows, dh), F32) + cb_ref[...]
    for k in range(M_CONV):
        acc = acc + cw_ref[k:k + 1, :] * xbuf[pl.ds(pad - (M_CONV - 1) + k, rows), :]
    xc = acc * _sigmoid(acc)
    xc_s[...] = xc
    xbuf[0:pad, :] = xblk[rows - pad:rows, :]

    scale = dh ** -0.5
    for t in range(dh // LANES):
        sl = slice(t * LANES, (t + 1) * LANES)
        xct = xc[:, sl].astype(BF16)
        xmt = xblk[:, sl].astype(BF16)
        q_s[:, sl] = jnp.dot(xct, wq_ref[t], preferred_element_type=F32).astype(BF16)
        k_s[:, sl] = jnp.dot(xct, wk_ref[t], preferred_element_type=F32) * scale
        v_s[:, sl] = jnp.dot(xmt, wv_ref[t], preferred_element_type=F32).astype(BF16)

    lane = lax.broadcasted_iota(jnp.int32, (L, LANES), 1)
    row_i = lax.broadcasted_iota(jnp.int32, (L, L), 0)
    col_i = lax.broadcasted_iota(jnp.int32, (L, L), 1)
    tril = col_i <= row_i
    eye = col_i == row_i

    def to_row(col):
        return jnp.sum(jnp.where(eye, col, 0.0), axis=0, keepdims=True)

    def chunk(c, carry):
        r0 = pl.multiple_of(c * L, L)
        rs = pl.ds(r0, L)
        g = sm_ref[rs, :] + gb_ref[...]
        i_col = jnp.sum(jnp.where(lane == h, g, 0.0), axis=1, keepdims=True)
        f_col = _log_sigmoid(jnp.sum(jnp.where(lane == n_heads + h, g, 0.0), axis=1, keepdims=True))
        i_row = to_row(i_col)
        f_row = to_row(f_col)
        b_col = jnp.sum(jnp.where(tril, f_row, 0.0), axis=1, keepdims=True)
        b_row = jnp.sum(jnp.where(col_i >= row_i, f_col, 0.0), axis=0, keepdims=True)
        m_prev = m_s[...]
        dmat = jnp.where(tril, b_col - b_row + i_row, -jnp.inf)
        m_inter = b_col + m_prev
        m_t = jnp.maximum(m_inter, jnp.max(dmat, axis=1, keepdims=True))
        qc = q_s[rs, :]
        kc = k_s[rs, :]
        vc = v_s[rs, :]
        s_qk = lax.dot_general(qc, kc.astype(BF16), (((1,), (1,)), ((), ())),
                               preferred_element_type=F32)
        a = s_qk * jnp.exp(dmat - m_t)
        inter = jnp.exp(m_inter - m_t)
        c_prev = c_s[...]
        num = (jnp.dot(a.astype(BF16), vc, preferred_element_type=F32)
               + inter * jnp.dot(qc, c_prev.astype(BF16), preferred_element_type=F32))
        qn = jnp.sum(qc.astype(F32) * n_s[...], axis=1, keepdims=True)
        den = jnp.sum(a, axis=1, keepdims=True) + inter * qn
        hh = num / jnp.maximum(jnp.abs(den), jnp.exp(-m_t))

        mu = jnp.mean(hh, axis=1, keepdims=True)
        hc = hh - mu
        hn = hc * lax.rsqrt(jnp.mean(hc * hc, axis=1, keepdims=True) + EPS) * gh_ref[...]
        hn = _sigmoid(om_ref[rs, :]) * hn
        zm = zm_ref[rs, :]
        y = (hn + sk_ref[...] * xc_s[rs, :]) * (zm * _sigmoid(zm))
        y_ref[rs, :] = y.astype(y_ref.dtype)

        b_last = b_col[L - 1:L, :]
        g_col = b_last - b_col + i_col
        m_new = jnp.maximum(b_last + m_prev, jnp.max(g_col, axis=0, keepdims=True))
        w_col = jnp.exp(g_col - m_new)
        decay = jnp.exp(b_last + m_prev - m_new)
        kw = kc * w_col
        c_s[...] = decay * c_prev + lax.dot_general(
            kw.astype(BF16), vc, (((0,), (0,)), ((), ())), preferred_element_type=F32)
        n_s[...] = decay * n_s[...] + jnp.sum(kw, axis=0, keepdims=True)
        m_s[...] = m_new
        return carry

    lax.fori_loop(0, rows // L, chunk, 0, unroll=True)


def _mlstm(proj3, col_xm, col_om, col_zm, col_small, conv_w, conv_b, wq_d, wk_d, wv_d,
           gate_bias, g_head, skip, n_heads, rows=256, chunk=256):
    B, S, _ = proj3.shape
    dm = conv_w.shape[1]
    dh = dm // n_heads
    rows = min(rows, S)
    chunk = min(chunk, rows)
    assert rows % chunk == 0 and S % rows == 0
    nt = dh // LANES
    bx, bo, bz, bs = col_xm // dh, col_om // dh, col_zm // dh, col_small // LANES
    seq = lambda base: pl.BlockSpec((None, rows, dh), lambda b, h, j: (b, j, base + h))
    vec = pl.BlockSpec((1, dh), lambda b, h, j: (0, h))
    wsp = pl.BlockSpec((nt, LANES, LANES), lambda b, h, j: (h, 0, 0))
    kern = functools.partial(_mlstm_kernel, n_heads=n_heads, rows=rows, chunk=chunk)
    return pl.pallas_call(
        kern,
        grid=(B, n_heads, S // rows),
        in_specs=[seq(bx), seq(bo), seq(bz),
                  pl.BlockSpec((None, rows, LANES), lambda b, h, j: (b, j, bs)),
                  pl.BlockSpec((M_CONV, dh), lambda b, h, j: (0, h)),
                  vec, wsp, wsp, wsp,
                  pl.BlockSpec((1, LANES), lambda b, h, j: (0, 0)),
                  vec, vec],
        out_specs=pl.BlockSpec((None, rows, dh), lambda b, h, j: (b, j, h)),
        out_shape=jax.ShapeDtypeStruct((B, S, dm), BF16),
        scratch_shapes=[pltpu.VMEM((rows + SUBLANES, dh), F32),
                        pltpu.VMEM((rows, dh), F32),
                        pltpu.VMEM((rows, dh), BF16),
                        pltpu.VMEM((rows, dh), F32),
                        pltpu.VMEM((rows, dh), BF16),
                        pltpu.VMEM((dh, dh), F32),
                        pltpu.VMEM((1, dh), F32),
                        pltpu.VMEM((1, 1), F32)],
        compiler_params=_cparams(("parallel", "parallel", "arbitrary")),
        name="mlstm",
    )(proj3, proj3, proj3, proj3, conv_w, conv_b.reshape(1, dm), wq_d, wk_d, wv_d,
      gate_bias, g_head.reshape(1, dm), skip.reshape(1, dm))


def _dsa_prep_kernel(ckv_ref, ki_ref, gc_ref, gk_ref, bk_ref, c_ref, ct_ref, kidx_ref):
    x = ckv_ref[...]
    c = x * lax.rsqrt(jnp.mean(x * x, axis=-1, keepdims=True) + EPS) * gc_ref[...]
    c_ref[...] = c.astype(c_ref.dtype)
    ct_ref[...] = c.T.astype(ct_ref.dtype)
    k = ki_ref[...]
    mu = jnp.mean(k, axis=-1, keepdims=True)
    kc = k - mu
    y = kc * lax.rsqrt(jnp.mean(kc * kc, axis=-1, keepdims=True) + EPS)
    kidx_ref[...] = (y * gk_ref[...] + bk_ref[...]).astype(kidx_ref.dtype)


def _dsa_prep(proj3, col_ckv, col_ki, g_ckv, g_kidx, b_kidx, tm=512):
    B, S, _ = proj3.shape
    dc, di = g_ckv.shape[0], g_kidx.shape[0]
    tm = min(tm, S)
    return pl.pallas_call(
        _dsa_prep_kernel,
        grid=(B, S // tm),
        in_specs=[pl.BlockSpec((None, tm, dc), lambda b, i: (b, i, col_ckv // dc)),
                  pl.BlockSpec((None, tm, di), lambda b, i: (b, i, col_ki // di)),
                  pl.BlockSpec((1, dc), lambda b, i: (0, 0)),
                  pl.BlockSpec((1, di), lambda b, i: (0, 0)),
                  pl.BlockSpec((1, di), lambda b, i: (0, 0))],
        out_specs=[pl.BlockSpec((None, tm, dc), lambda b, i: (b, i, 0)),
                   pl.BlockSpec((None, dc, tm), lambda b, i: (b, 0, i)),
                   pl.BlockSpec((None, tm, di), lambda b, i: (b, i, 0))],
        out_shape=[jax.ShapeDtypeStruct((B, S, dc), BF16),
                   jax.ShapeDtypeStruct((B, dc, S), BF16),
                   jax.ShapeDtypeStruct((B, S, di), BF16)],
        compiler_params=_cparams(("parallel", "parallel")),
        name="dsa_prep",
    )(proj3, proj3, g_ckv.reshape(1, dc), g_kidx.reshape(1, di), b_kidx.reshape(1, di))


def _indexer_kernel(qi_ref, sm_ref, kidx_ref, mask_ref, score_s, key_s, wt_s,
                    *, n_idx, wi_lane, topk, q0):
    lk, di = kidx_ref.shape
    qg = qi_ref.shape[0]
    kf = float(topk)
    wt_s[...] = sm_ref[...].T * ((n_idx ** -0.5) * (di ** -0.5))
    kidx = kidx_ref[...]
    score_s[...] = jnp.zeros_like(score_s)

    def head(hh, carry):
        c0 = pl.multiple_of(hh * di, di)
        qh = qi_ref[:, pl.ds(c0, di)].astype(BF16)
        logits = lax.dot_general(kidx, qh, (((1,), (1,)), ((), ())), preferred_element_type=F32)
        w = wt_s[pl.ds(wi_lane + hh, 1), :]
        score_s[...] += jnp.maximum(logits, 0.0) * w
        return carry

    lax.fori_loop(0, n_idx, head, 0, unroll=2)

    bits = pltpu.bitcast(score_s[...], jnp.int32)
    key = jnp.where(bits >= 0, bits, bits ^ jnp.int32(0x7FFFFFFF))
    s_pos = lax.broadcasted_iota(jnp.int32, (lk, qg), 0)
    t_pos = q0 + lax.broadcasted_iota(jnp.int32, (lk, qg), 1)
    causal = s_pos <= t_pos
    key_s[...] = jnp.where(causal, key, jnp.int32(INT_MIN))

    def count(pred):
        return jnp.sum(jnp.where(pred, 1.0, 0.0), axis=0, keepdims=True)

    thr = jnp.where(count(key_s[...] >= 0) >= kf, jnp.int32(0), jnp.int32(INT_MIN))

    def bit_step(it, thr):
        cand = thr + jnp.left_shift(jnp.int32(1), 30 - it)
        return jnp.where(count(key_s[...] >= cand) >= kf, cand, thr)

    thr = lax.fori_loop(0, 31, bit_step, thr)
    key = key_s[...]
    ge = key >= thr
    mask_ref[...] = jnp.where(jnp.logical_and(ge, causal), 1, 0).astype(mask_ref.dtype)

    excess = jnp.logical_and(count(ge) > kf, thr > jnp.int32(INT_MIN))

    @pl.when(jnp.max(jnp.where(excess, 1.0, 0.0)) > 0.0)
    def _():
        gt = key > thr
        tie = key == thr
        need = kf - count(gt)
        nbits = max(1, (lk - 1).bit_length())

        def idx_step(it, x):
            cand = x + jnp.left_shift(jnp.int32(1), nbits - 1 - it)
            c = count(jnp.logical_and(tie, s_pos < cand))
            return jnp.where(c < need, cand, x)

        x = lax.fori_loop(0, nbits, idx_step, jnp.zeros((1, qg), jnp.int32))
        sel = jnp.logical_or(gt, jnp.logical_and(tie, s_pos <= x))
        mask_ref[...] = jnp.where(jnp.logical_and(sel, causal), 1, 0).astype(mask_ref.dtype)


def _indexer(proj3, col_qi, col_small, kidx3, grp, n_idx, wi_lane, topk):
    B, S, di = kidx3.shape
    wq = n_idx * di
    lk = (grp + 1) * Q_GROUP
    kern = functools.partial(_indexer_kernel, n_idx=n_idx, wi_lane=wi_lane, topk=topk,
                             q0=grp * Q_GROUP)
    return pl.pallas_call(
        kern,
        grid=(B,),
        in_specs=[pl.BlockSpec((None, Q_GROUP, wq), lambda b: (b, grp, col_qi // wq)),
                  pl.BlockSpec((None, Q_GROUP, LANES), lambda b: (b, grp, col_small // LANES)),
                  pl.BlockSpec((None, lk, di), lambda b: (b, 0, 0))],
        out_specs=pl.BlockSpec((None, lk, Q_GROUP), lambda b: (b, 0, 0)),
        out_shape=jax.ShapeDtypeStruct((B, lk, Q_GROUP), jnp.int8),
        scratch_shapes=[pltpu.VMEM((lk, Q_GROUP), F32),
                        pltpu.VMEM((lk, Q_GROUP), jnp.int32),
                        pltpu.VMEM((LANES, Q_GROUP), F32)],
        compiler_params=_cparams(("parallel",)),
        name="indexer",
    )(proj3, proj3, kidx3)


def _attn_kernel(qa_ref, za_ref, mask_ref, c_ref, ct_ref, wuk_ref, wuvt_ref, y_all_ref, y_ref,
                 qlat_s, bias_s, s_s, *, heads, kt):
    del y_all_ref
    lk, dc = c_ref.shape
    dh = wuk_ref.shape[2]
    qb = qa_ref.shape[0]
    R = heads * qb
    assert lk % kt == 0 and kt % SUBLANES == 0
    scale = (dh ** -0.5) * math.log2(math.e)
    nt = (((1,), (1,)), ((), ()))

    for g in range(heads):
        qh = qa_ref[:, g * dh:(g + 1) * dh].astype(BF16)
        ql = lax.dot_general(wuk_ref[g], qh, nt, preferred_element_type=F32)
        qlat_s[:, g * qb:(g + 1) * qb] = (ql * scale).astype(BF16)
    bias_s[...] = jnp.where(mask_ref[...].astype(jnp.int32) != 0, 0.0, -jnp.inf)

    m8 = jnp.full((SUBLANES, R), -jnp.inf, F32)
    for i in range(lk // kt):
        ks = slice(i * kt, (i + 1) * kt)
        st = jnp.dot(c_ref[ks, :], qlat_s[...], preferred_element_type=F32)
        b = bias_s[ks, :]
        tops = []
        for g in range(heads):
            blk = st[:, g * qb:(g + 1) * qb] + b
            s_s[ks, g * qb:(g + 1) * qb] = blk
            tops.append(jnp.max(blk.reshape(kt // SUBLANES, SUBLANES, qb), axis=0))
        m8 = jnp.maximum(m8, jnp.concatenate(tops, axis=1))
    m = jnp.max(m8, axis=0, keepdims=True)

    l8 = jnp.zeros((SUBLANES, R), F32)
    olat = jnp.zeros((dc, R), F32)
    for i in range(lk // kt):
        ks = slice(i * kt, (i + 1) * kt)
        p = jnp.exp2(s_s[ks, :] - m)
        l8 = l8 + jnp.sum(p.reshape(kt // SUBLANES, SUBLANES, R), axis=0)
        olat = olat + jnp.dot(ct_ref[:, ks], p.astype(BF16), preferred_element_type=F32)
    l = jnp.sum(l8, axis=0, keepdims=True)
    olat = (olat * (1.0 / l)).astype(BF16)
    for g in range(heads):
        ot = jnp.dot(wuvt_ref[g], olat[:, g * qb:(g + 1) * qb], preferred_element_type=F32)
        za = za_ref[:, g * dh:(g + 1) * dh]
        y_ref[:, g * dh:(g + 1) * dh] = (ot.T * (za * _sigmoid(za))).astype(y_ref.dtype)


def _attention(proj3, col_qa, col_za, mask, c3, ct3, wuk, wuvt, y_all, grp, heads_per_step=16, kt=512):
    B, S, dc = c3.shape
    n_heads, _, dh = wuk.shape
    G = min(heads_per_step, n_heads)
    gw = G * dh
    lk = (grp + 1) * Q_GROUP
    nq = Q_GROUP // Q_BLOCK
    R = G * Q_BLOCK
    kt = math.gcd(kt, lk)
    kern = functools.partial(_attn_kernel, heads=G, kt=kt)
    return pl.pallas_call(
        kern,
        grid=(B, nq, n_heads // G),
        in_specs=[pl.BlockSpec((None, Q_BLOCK, gw), lambda b, q, g: (b, grp * nq + q, col_qa // gw + g)),
                  pl.BlockSpec((None, Q_BLOCK, gw), lambda b, q, g: (b, grp * nq + q, col_za // gw + g)),
                  pl.BlockSpec((None, lk, Q_BLOCK), lambda b, q, g: (b, 0, q)),
                  pl.BlockSpec((None, lk, dc), lambda b, q, g: (b, 0, 0)),
                  pl.BlockSpec((None, dc, lk), lambda b, q, g: (b, 0, 0)),
                  pl.BlockSpec((G, dc, dh), lambda b, q, g: (g, 0, 0)),
                  pl.BlockSpec((G, dh, dc), lambda b, q, g: (g, 0, 0)),
                  pl.BlockSpec(memory_space=pl.ANY)],
        out_specs=pl.BlockSpec((None, Q_BLOCK, gw), lambda b, q, g: (b, grp * nq + q, g)),
        out_shape=jax.ShapeDtypeStruct(y_all.shape, y_all.dtype),
        input_output_aliases={7: 0},
        scratch_shapes=[pltpu.VMEM((dc, R), BF16),
                        pltpu.VMEM((lk, Q_BLOCK), F32),
                        pltpu.VMEM((lk, R), F32)],
        compiler_params=_cparams(("parallel", "parallel", "parallel")),
        name="attention",
    )(proj3, proj3, mask, c3, ct3, wuk, wuvt, y_all)


def _blockdiag_dense(w):
    nblk, blk, _ = w.shape
    per = LANES // blk
    w4 = w.reshape(nblk // per, per, blk, blk)
    eye = jnp.eye(per, dtype=w.dtype)
    dense = jnp.einsum('tgio,gh->tgiho', w4, eye)
    return dense.reshape(nblk // per, LANES, LANES).astype(BF16)


def kernel(x, g_norm, w_in, conv_w, conv_b, w_q_m, w_k_m, w_v_m, b_i, b_f, g_head_m, skip_m, g_ckv, w_uk, w_uv, g_kidx, b_kidx, w_bm, w_ba, w_out, g_final):
    B, S, D = x.shape
    depth = g_norm.shape[0]
    T = B * S
    dm = conv_w.shape[-1]
    mh = b_i.shape[-1]
    ah, dc, adh = w_uk.shape[1:]
    da = ah * adh
    di = g_kidx.shape[-1]
    n_in = w_in.shape[-1]
    n_idx = (n_in - (3 * dm + 2 * mh + da + dc + da + di + 2 * D)) // (di + 1)
    topk = min(IDX_TOPK, S // 4)
    assert 2 * mh + n_idx <= LANES and S % Q_GROUP == 0

    sizes = (dm, dm, dm, mh, mh, da, dc, da, n_idx * di, di, n_idx, D, D)
    offs = [0]
    for s in sizes:
        offs.append(offs[-1] + s)
    (o_xm, o_om, o_zm, o_ip, o_fp, o_qa, o_ckv, o_za, o_qi, o_ki, o_wi, o_gm, o_ga) = offs[:-1]

    order = [(o_xm, dm), (o_om, dm), (o_zm, dm), (o_qa, da), (o_za, da), (o_qi, n_idx * di),
             (o_gm, D), (o_ga, D), (o_ckv, dc), (o_ki, di)]
    cols, pos = {}, 0
    for off, width in order:
        cols[off] = pos
        pos += width
    c_small = pos
    pos += LANES
    tn = 1024
    n_pad = -(-pos // tn) * tn

    runs = []
    for off, width in order:
        if runs and runs[-1][0] + runs[-1][1] == off and runs[-1][2] + runs[-1][1] == cols[off]:
            runs[-1] = (runs[-1][0], runs[-1][1] + width, runs[-1][2])
        else:
            runs.append((off, width, cols[off]))

    w_in_rows = w_in.reshape(depth * D, n_in)
    w_bm_rows = w_bm.reshape(depth * dm, D)
    w_ba_rows = w_ba.reshape(depth * da, D)
    w_out_rows = w_out.reshape(depth * D, D)

    x2 = x.reshape(T, D)
    for l in range(depth):
        w_cat = jnp.zeros((D, n_pad), BF16)
        for off, width, dst in runs:
            w_cat = _relayout(w_in_rows, l, off, width, w_cat, dst)
        small = jnp.concatenate([w_in[l, :, o_ip:o_ip + mh], w_in[l, :, o_fp:o_fp + mh],
                                 w_in[l, :, o_wi:o_wi + n_idx],
                                 jnp.zeros((D, LANES - 2 * mh - n_idx), F32)], axis=1)
        w_cat = _relayout(small, 0, 0, LANES, w_cat, c_small)

        xn = _rmsnorm(x2, g_norm[l], BF16)
        proj = _matmul(xn, w_cat, F32, tn=tn)
        proj3 = proj.reshape(B, S, n_pad)

        gate_bias = jnp.concatenate([b_i[l], b_f[l], jnp.zeros((LANES - 2 * mh,), F32)]).reshape(1, LANES)
        y_m = _mlstm(proj3, cols[o_xm], cols[o_om], cols[o_zm], c_small, conv_w[l], conv_b[l],
                     _blockdiag_dense(w_q_m[l]), _blockdiag_dense(w_k_m[l]), _blockdiag_dense(w_v_m[l]),
                     gate_bias, g_head_m[l], skip_m[l], mh)

        c3, ct3, kidx3 = _dsa_prep(proj3, cols[o_ckv], cols[o_ki], g_ckv[l], g_kidx[l], b_kidx[l])
        wuk = w_uk[l].astype(BF16)
        wuvt = jnp.swapaxes(w_uv[l], 1, 2).astype(BF16)
        y_a = jnp.zeros((B, S, da), BF16)
        for grp in range(S // Q_GROUP):
            mask = _indexer(proj3, cols[o_qi], c_small, kidx3, grp, n_idx, 2 * mh, topk)
            y_a = _attention(proj3, cols[o_qa], cols[o_za], mask, c3, ct3, wuk, wuvt, y_a, grp)

        merged = _merge(y_m.reshape(T, dm), y_a.reshape(T, da), _to_bf16(w_bm_rows, l, dm),
                        _to_bf16(w_ba_rows, l, da), proj, cols[o_gm], cols[o_ga])
        x2 = _out_proj(merged, _to_bf16(w_out_rows, l, D), x2)
    out = _rmsnorm(x2, g_final, F32)
    return out.reshape(B, S, D)
```

```python
import functools
import math

import jax
import jax.numpy as jnp
from jax import lax
from jax.experimental import pallas as pl
from jax.experimental.pallas import tpu as pltpu

EPS = 1e-6
M_QKV_BLOCK = 4
M_CONV = 4
IDX_TOPK = 256
Q_BLOCK = 128
Q_GROUP = 256
LANES = 128
SUBLANES = 8
VMEM_LIMIT = 56 * 1024 * 1024

F32 = jnp.float32
BF16 = jnp.bfloat16
INT_MIN = -2 ** 31


def _cparams(sem):
    return pltpu.CompilerParams(dimension_semantics=sem, vmem_limit_bytes=VMEM_LIMIT)


def _rmsnorm_kernel(x_ref, g_ref, o_ref):
    x = x_ref[...]
    y = x * lax.rsqrt(jnp.mean(x * x, axis=-1, keepdims=True) + EPS)
    o_ref[...] = (y * g_ref[...]).astype(o_ref.dtype)


def _rmsnorm(x2d, g, out_dtype, tm=256):
    T, D = x2d.shape
    tm = min(tm, T)
    return pl.pallas_call(
        _rmsnorm_kernel,
        grid=(T // tm,),
        in_specs=[pl.BlockSpec((tm, D), lambda i: (i, 0)),
                  pl.BlockSpec((1, D), lambda i: (0, 0))],
        out_specs=pl.BlockSpec((tm, D), lambda i: (i, 0)),
        out_shape=jax.ShapeDtypeStruct((T, D), out_dtype),
        compiler_params=_cparams(("parallel",)),
        name="rmsnorm",
    )(x2d, g.reshape(1, D))


def _cast_kernel(x_ref, o_ref):
    o_ref[...] = x_ref[...].astype(o_ref.dtype)


def _to_bf16(w_stack, layer, rows, tm=512):
    N = w_stack.shape[1]
    tm = min(tm, rows)
    nb = rows // tm
    return pl.pallas_call(
        _cast_kernel,
        grid=(nb,),
        in_specs=[pl.BlockSpec((tm, N), lambda i: (layer * nb + i, 0))],
        out_specs=pl.BlockSpec((tm, N), lambda i: (i, 0)),
        out_shape=jax.ShapeDtypeStruct((rows, N), BF16),
        compiler_params=_cparams(("parallel",)),
        name="cast_bf16",
    )(w_stack)


def _mm_kernel(a_ref, w_ref, o_ref):
    o_ref[...] = jnp.dot(a_ref[...], w_ref[...], preferred_element_type=F32).astype(o_ref.dtype)


def _matmul(a, w, out_dtype, tm=1024, tn=512):
    M, K = a.shape
    N = w.shape[1]
    tm, tn = min(tm, M), min(tn, N)
    return pl.pallas_call(
        _mm_kernel,
        grid=(M // tm, N // tn),
        in_specs=[pl.BlockSpec((tm, K), lambda i, j: (i, 0)),
                  pl.BlockSpec((K, tn), lambda i, j: (0, j))],
        out_specs=pl.BlockSpec((tm, tn), lambda i, j: (i, j)),
        out_shape=jax.ShapeDtypeStruct((M, N), out_dtype),
        compiler_params=_cparams(("parallel", "parallel")),
        name="in_proj",
    )(a, w)


def _merge_kernel(ym_ref, ya_ref, wbm_ref, wba_ref, gm_ref, ga_ref, o_ref):
    half = o_ref.shape[1] // 2
    for c in range(2):
        cs = slice(c * half, (c + 1) * half)
        pm = jnp.dot(ym_ref[...], wbm_ref[:, cs], preferred_element_type=F32)
        pa = jnp.dot(ya_ref[...], wba_ref[:, cs], preferred_element_type=F32)
        o = _sigmoid(gm_ref[:, cs]) * pm + _sigmoid(ga_ref[:, cs]) * pa
        o_ref[:, cs] = o.astype(o_ref.dtype)


def _merge(ym, ya, wbm, wba, proj, gm_col, ga_col, tm=512, tn=512):
    M, K = ym.shape
    N = wbm.shape[1]
    tm, tn = min(tm, M), min(tn, N)
    gm_blk, ga_blk = gm_col // tn, ga_col // tn
    return pl.pallas_call(
        _merge_kernel,
        grid=(M // tm, N // tn),
        in_specs=[pl.BlockSpec((tm, K), lambda i, j: (i, 0)),
                  pl.BlockSpec((tm, K), lambda i, j: (i, 0)),
                  pl.BlockSpec((K, tn), lambda i, j: (0, j)),
                  pl.BlockSpec((K, tn), lambda i, j: (0, j)),
                  pl.BlockSpec((tm, tn), lambda i, j: (i, gm_blk + j)),
                  pl.BlockSpec((tm, tn), lambda i, j: (i, ga_blk + j))],
        out_specs=pl.BlockSpec((tm, tn), lambda i, j: (i, j)),
        out_shape=jax.ShapeDtypeStruct((M, N), BF16),
        compiler_params=_cparams(("parallel", "parallel")),
        name="merge",
    )(ym, ya, wbm, wba, proj, proj)


def _out_kernel(a_ref, w_ref, x_ref, o_ref):
    o_ref[...] = x_ref[...] + jnp.dot(a_ref[...], w_ref[...], preferred_element_type=F32)


def _out_proj(a, w, x2d, tm=1024, tn=512):
    M, K = a.shape
    N = w.shape[1]
    tm, tn = min(tm, M), min(tn, N)
    return pl.pallas_call(
        _out_kernel,
        grid=(M // tm, N // tn),
        in_specs=[pl.BlockSpec((tm, K), lambda i, j: (i, 0)),
                  pl.BlockSpec((K, tn), lambda i, j: (0, j)),
                  pl.BlockSpec((tm, tn), lambda i, j: (i, j))],
        out_specs=pl.BlockSpec((tm, tn), lambda i, j: (i, j)),
        out_shape=jax.ShapeDtypeStruct((M, N), F32),
        compiler_params=_cparams(("parallel", "parallel")),
        name="out_proj",
    )(a, w, x2d)


def _sigmoid(x):
    return 0.5 * jnp.tanh(0.5 * x) + 0.5


def _log_sigmoid(x):
    return -(jnp.maximum(-x, 0.0) + jnp.log1p(jnp.exp(-jnp.abs(x))))


def _mlstm_kernel(xm_ref, om_ref, zm_ref, sm_ref, cw_ref, cb_ref, wq_ref, wk_ref, wv_ref,
                  gb_ref, gh_ref, sk_ref, y_ref,
                  xbuf, xc_s, q_s, k_s, v_s, c_s, n_s, m_s, *, n_heads, rows, chunk):
    h = pl.program_id(1)
    j = pl.program_id(2)
    dh = xc_s.shape[1]
    L = chunk
    pad = SUBLANES

    @pl.when(j == 0)
    def _():
        xbuf[0:pad, :] = jnp.zeros((pad, dh), F32)
        c_s[...] = jnp.zeros_like(c_s)
        n_s[...] = jnp.zeros_like(n_s)
        m_s[...] = jnp.zeros_like(m_s)

    xblk = xm_ref[...]
    xbuf[pad:pad + rows, :] = xblk
    acc = jnp.zeros((rows, dh), F32) + cb_ref[...]
    for k in range(M_CONV):
        acc = acc + cw_ref[k:k + 1, :] * xbuf[pl.ds(pad - (M_CONV - 1) + k, rows), :]
    xc = acc * _sigmoid(acc)
    xc_s[...] = xc
    xbuf[0:pad, :] = xblk[rows - pad:rows, :]

    scale = dh ** -0.5
    for t in range(dh // LANES):
        sl = slice(t * LANES, (t + 1) * LANES)
        xct = xc[:, sl].astype(BF16)
        xmt = xblk[:, sl].astype(BF16)
        q_s[:, sl] = jnp.dot(xct, wq_ref[t], preferred_element_type=F32).astype(BF16)
        k_s[:, sl] = jnp.dot(xct, wk_ref[t], preferred_element_type=F32) * scale
        v_s[:, sl] = jnp.dot(xmt, wv_ref[t], preferred_element_type=F32).astype(BF16)

    lane = lax.broadcasted_iota(jnp.int32, (L, LANES), 1)
    row_i = lax.broadcasted_iota(jnp.int32, (L, L), 0)
    col_i = lax.broadcasted_iota(jnp.int32, (L, L), 1)
    tril = col_i <= row_i
    eye = col_i == row_i

    def to_row(col):
        return jnp.sum(jnp.where(eye, col, 0.0), axis=0, keepdims=True)

    def chunk(c, carry):
        r0 = pl.multiple_of(c * L, L)
        rs = pl.ds(r0, L)
        g = sm_ref[rs, :] + gb_ref[...]
        i_col = jnp.sum(jnp.where(lane == h, g, 0.0), axis=1, keepdims=True)
        f_col = _log_sigmoid(jnp.sum(jnp.where(lane == n_heads + h, g, 0.0), axis=1, keepdims=True))
        i_row = to_row(i_col)
        f_row = to_row(f_col)
        b_col = jnp.sum(jnp.where(tril, f_row, 0.0), axis=1, keepdims=True)
        b_row = jnp.sum(jnp.where(col_i >= row_i, f_col, 0.0), axis=0, keepdims=True)
        m_prev = m_s[...]
        dmat = jnp.where(tril, b_col - b_row + i_row, -jnp.inf)
        m_inter = b_col + m_prev
        m_t = jnp.maximum(m_inter, jnp.max(dmat, axis=1, keepdims=True))
        qc = q_s[rs, :]
        kc = k_s[rs, :]
        vc = v_s[rs, :]
        s_qk = lax.dot_general(qc, kc.astype(BF16), (((1,), (1,)), ((), ())),
                               preferred_element_type=F32)
        a = s_qk * jnp.exp(dmat - m_t)
        inter = jnp.exp(m_inter - m_t)
        c_prev = c_s[...]
        num = (jnp.dot(a.astype(BF16), vc, preferred_element_type=F32)
               + inter * jnp.dot(qc, c_prev.astype(BF16), preferred_element_type=F32))
        qn = jnp.sum(qc.astype(F32) * n_s[...], axis=1, keepdims=True)
        den = jnp.sum(a, axis=1, keepdims=True) + inter * qn
        hh = num / jnp.maximum(jnp.abs(den), jnp.exp(-m_t))

        mu = jnp.mean(hh, axis=1, keepdims=True)
        hc = hh - mu
        hn = hc * lax.rsqrt(jnp.mean(hc * hc, axis=1, keepdims=True) + EPS) * gh_ref[...]
        hn = _sigmoid(om_ref[rs, :]) * hn
        zm = zm_ref[rs, :]
        y = (hn + sk_ref[...] * xc_s[rs, :]) * (zm * _sigmoid(zm))
        y_ref[rs, :] = y.astype(y_ref.dtype)

        b_last = b_col[L - 1:L, :]
        g_col = b_last - b_col + i_col
        m_new = jnp.maximum(b_last + m_prev, jnp.max(g_col, axis=0, keepdims=True))
        w_col = jnp.exp(g_col - m_new)
        decay = jnp.exp(b_last + m_prev - m_new)
        kw = kc * w_col
        c_s[...] = decay * c_prev + lax.dot_general(
            kw.astype(BF16), vc, (((0,), (0,)), ((), ())), preferred_element_type=F32)
        n_s[...] = decay * n_s[...] + jnp.sum(kw, axis=0, keepdims=True)
        m_s[...] = m_new
        return carry

    lax.fori_loop(0, rows // L, chunk, 0, unroll=True)


def _mlstm(proj3, col_xm, col_om, col_zm, col_small, conv_w, conv_b, wq_d, wk_d, wv_d,
           gate_bias, g_head, skip, n_heads, rows=256, chunk=256):
    B, S, _ = proj3.shape
    dm = conv_w.shape[1]
    dh = dm // n_heads
    rows = min(rows, S)
    chunk = min(chunk, rows)
    assert rows % chunk == 0 and S % rows == 0
    nt = dh // LANES
    bx, bo, bz, bs = col_xm // dh, col_om // dh, col_zm // dh, col_small // LANES
    seq = lambda base: pl.BlockSpec((None, rows, dh), lambda b, h, j: (b, j, base + h))
    vec = pl.BlockSpec((1, dh), lambda b, h, j: (0, h))
    wsp = pl.BlockSpec((nt, LANES, LANES), lambda b, h, j: (h, 0, 0))
    kern = functools.partial(_mlstm_kernel, n_heads=n_heads, rows=rows, chunk=chunk)
    return pl.pallas_call(
        kern,
        grid=(B, n_heads, S // rows),
        in_specs=[seq(bx), seq(bo), seq(bz),
                  pl.BlockSpec((None, rows, LANES), lambda b, h, j: (b, j, bs)),
                  pl.BlockSpec((M_CONV, dh), lambda b, h, j: (0, h)),
                  vec, wsp, wsp, wsp,
                  pl.BlockSpec((1, LANES), lambda b, h, j: (0, 0)),
                  vec, vec],
        out_specs=pl.BlockSpec((None, rows, dh), lambda b, h, j: (b, j, h)),
        out_shape=jax.ShapeDtypeStruct((B, S, dm), BF16),
        scratch_shapes=[pltpu.VMEM((rows + SUBLANES, dh), F32),
                        pltpu.VMEM((rows, dh), F32),
                        pltpu.VMEM((rows, dh), BF16),
                        pltpu.VMEM((rows, dh), F32),
                        pltpu.VMEM((rows, dh), BF16),
                        pltpu.VMEM((dh, dh), F32),
                        pltpu.VMEM((1, dh), F32),
                        pltpu.VMEM((1, 1), F32)],
        compiler_params=_cparams(("parallel", "parallel", "arbitrary")),
        name="mlstm",
    )(proj3, proj3, proj3, proj3, conv_w, conv_b.reshape(1, dm), wq_d, wk_d, wv_d,
      gate_bias, g_head.reshape(1, dm), skip.reshape(1, dm))


def _dsa_prep_kernel(ckv_ref, ki_ref, gc_ref, gk_ref, bk_ref, c_ref, ct_ref, kidx_ref):
    x = ckv_ref[...]
    c = x * lax.rsqrt(jnp.mean(x * x, axis=-1, keepdims=True) + EPS) * gc_ref[...]
    c_ref[...] = c.astype(c_ref.dtype)
    ct_ref[...] = c.T.astype(ct_ref.dtype)
    k = ki_ref[...]
    mu = jnp.mean(k, axis=-1, keepdims=True)
    kc = k - mu
    y = kc * lax.rsqrt(jnp.mean(kc * kc, axis=-1, keepdims=True) + EPS)
    kidx_ref[...] = (y * gk_ref[...] + bk_ref[...]).astype(kidx_ref.dtype)


def _dsa_prep(proj3, col_ckv, col_ki, g_ckv, g_kidx, b_kidx, tm=512):
    B, S, _ = proj3.shape
    dc, di = g_ckv.shape[0], g_kidx.shape[0]
    tm = min(tm, S)
    return pl.pallas_call(
        _dsa_prep_kernel,
        grid=(B, S // tm),
        in_specs=[pl.BlockSpec((None, tm, dc), lambda b, i: (b, i, col_ckv // dc)),
                  pl.BlockSpec((None, tm, di), lambda b, i: (b, i, col_ki // di)),
                  pl.BlockSpec((1, dc), lambda b, i: (0, 0)),
                  pl.BlockSpec((1, di), lambda b, i: (0, 0)),
                  pl.BlockSpec((1, di), lambda b, i: (0, 0))],
        out_specs=[pl.BlockSpec((None, tm, dc), lambda b, i: (b, i, 0)),
                   pl.BlockSpec((None, dc, tm), lambda b, i: (b, 0, i)),
                   pl.BlockSpec((None, tm, di), lambda b, i: (b, i, 0))],
        out_shape=[jax.ShapeDtypeStruct((B, S, dc), BF16),
                   jax.ShapeDtypeStruct((B, dc, S), BF16),
                   jax.ShapeDtypeStruct((B, S, di), BF16)],
        compiler_params=_cparams(("parallel", "parallel")),
        name="dsa_prep",
    )(proj3, proj3, g_ckv.reshape(1, dc), g_kidx.reshape(1, di), b_kidx.reshape(1, di))


def _indexer_kernel(qi_ref, sm_ref, kidx_ref, mask_ref, score_s, key_s, wt_s,
                    *, n_idx, wi_lane, topk, q0):
    lk, di = kidx_ref.shape
    qg = qi_ref.shape[0]
    kf = float(topk)
    wt_s[...] = sm_ref[...].T * ((n_idx ** -0.5) * (di ** -0.5))
    kidx = kidx_ref[...]
    score_s[...] = jnp.zeros_like(score_s)

    def head(hh, carry):
        c0 = pl.multiple_of(hh * di, di)
        qh = qi_ref[:, pl.ds(c0, di)].astype(BF16)
        logits = lax.dot_general(kidx, qh, (((1,), (1,)), ((), ())), preferred_element_type=F32)
        w = wt_s[pl.ds(wi_lane + hh, 1), :]
        score_s[...] += jnp.maximum(logits, 0.0) * w
        return carry

    lax.fori_loop(0, n_idx, head, 0, unroll=2)

    bits = pltpu.bitcast(score_s[...], jnp.int32)
    key = jnp.where(bits >= 0, bits, bits ^ jnp.int32(0x7FFFFFFF))
    s_pos = lax.broadcasted_iota(jnp.int32, (lk, qg), 0)
    t_pos = q0 + lax.broadcasted_iota(jnp.int32, (lk, qg), 1)
    causal = s_pos <= t_pos
    key_s[...] = jnp.where(causal, key, jnp.int32(INT_MIN))

    def count(pred):
        return jnp.sum(jnp.where(pred, 1.0, 0.0), axis=0, keepdims=True)

    thr = jnp.where(count(key_s[...] >= 0) >= kf, jnp.int32(0), jnp.int32(INT_MIN))

    def bit_step(it, thr):
        cand = thr + jnp.left_shift(jnp.int32(1), 30 - it)
        return jnp.where(count(key_s[...] >= cand) >= kf, cand, thr)

    thr = lax.fori_loop(0, 31, bit_step, thr)
    key = key_s[...]
    ge = key >= thr
    mask_ref[...] = jnp.where(jnp.logical_and(ge, causal), 1, 0).astype(mask_ref.dtype)

    excess = jnp.logical_and(count(ge) > kf, thr > jnp.int32(INT_MIN))

    @pl.when(jnp.max(jnp.where(excess, 1.0, 0.0)) > 0.0)
    def _():
        gt = key > thr
        tie = key == thr
        need = kf - count(gt)
        nbits = max(1, (lk - 1).bit_length())

        def idx_step(it, x):
            cand = x + jnp.left_shift(jnp.int32(1), nbits - 1 - it)
            c = count(jnp.logical_and(tie, s_pos < cand))
            return jnp.where(c < need, cand, x)

        x = lax.fori_loop(0, nbits, idx_step, jnp.zeros((1, qg), jnp.int32))
        sel = jnp.logical_or(gt, jnp.logical_and(tie, s_pos <= x))
        mask_ref[...] = jnp.where(jnp.logical_and(sel, causal), 1, 0).astype(mask_ref.dtype)


def _indexer(proj3, col_qi, col_small, kidx3, grp, n_idx, wi_lane, topk):
    B, S, di = kidx3.shape
    wq = n_idx * di
    lk = (grp + 1) * Q_GROUP
    kern = functools.partial(_indexer_kernel, n_idx=n_idx, wi_lane=wi_lane, topk=topk,
                             q0=grp * Q_GROUP)
    return pl.pallas_call(
        kern,
        grid=(B,),
        in_specs=[pl.BlockSpec((None, Q_GROUP, wq), lambda b: (b, grp, col_qi // wq)),
                  pl.BlockSpec((None, Q_GROUP, LANES), lambda b: (b, grp, col_small // LANES)),
                  pl.BlockSpec((None, lk, di), lambda b: (b, 0, 0))],
        out_specs=pl.BlockSpec((None, lk, Q_GROUP), lambda b: (b, 0, 0)),
        out_shape=jax.ShapeDtypeStruct((B, lk, Q_GROUP), jnp.int8),
        scratch_shapes=[pltpu.VMEM((lk, Q_GROUP), F32),
                        pltpu.VMEM((lk, Q_GROUP), jnp.int32),
                        pltpu.VMEM((LANES, Q_GROUP), F32)],
        compiler_params=_cparams(("parallel",)),
        name="indexer",
    )(proj3, proj3, kidx3)


def _attn_kernel(qa_ref, za_ref, mask_ref, c_ref, ct_ref, wuk_ref, wuvt_ref, y_all_ref, y_ref,
                 qlat_s, bias_s, s_s, *, heads, kt):
    del y_all_ref
    lk, dc = c_ref.shape
    dh = wuk_ref.shape[2]
    qb = qa_ref.shape[0]
    R = heads * qb
    assert lk % kt == 0 and kt % SUBLANES == 0
    scale = (dh ** -0.5) * math.log2(math.e)
    nt = (((1,), (1,)), ((), ()))

    for g in range(heads):
        qh = qa_ref[:, g * dh:(g + 1) * dh].astype(BF16)
        ql = lax.dot_general(wuk_ref[g], qh, nt, preferred_element_type=F32)
        qlat_s[:, g * qb:(g + 1) * qb] = (ql * scale).astype(BF16)
    bias_s[...] = jnp.where(mask_ref[...].astype(jnp.int32) != 0, 0.0, -jnp.inf)

    m8 = jnp.full((SUBLANES, R), -jnp.inf, F32)
    for i in range(lk // kt):
        ks = slice(i * kt, (i + 1) * kt)
        st = jnp.dot(c_ref[ks, :], qlat_s[...], preferred_element_type=F32)
        b = bias_s[ks, :]
        tops = []
        for g in range(heads):
            blk = st[:, g * qb:(g + 1) * qb] + b
            s_s[ks, g * qb:(g + 1) * qb] = blk
            tops.append(jnp.max(blk.reshape(kt // SUBLANES, SUBLANES, qb), axis=0))
        m8 = jnp.maximum(m8, jnp.concatenate(tops, axis=1))
    m = jnp.max(m8, axis=0, keepdims=True)

    l8 = jnp.zeros((SUBLANES, R), F32)
    olat = jnp.zeros((dc, R), F32)
    for i in range(lk // kt):
        ks = slice(i * kt, (i + 1) * kt)
        p = jnp.exp2(s_s[ks, :] - m)
        l8 = l8 + jnp.sum(p.reshape(kt // SUBLANES, SUBLANES, R), axis=0)
        olat = olat + jnp.dot(ct_ref[:, ks], p.astype(BF16), preferred_element_type=F32)
    l = jnp.sum(l8, axis=0, keepdims=True)
    olat = (olat * (1.0 / l)).astype(BF16)
    for g in range(heads):
        ot = jnp.dot(wuvt_ref[g], olat[:, g * qb:(g + 1) * qb], preferred_element_type=F32)
        za = za_ref[:, g * dh:(g + 1) * dh]
        y_ref[:, g * dh:(g + 1) * dh] = (ot.T * (za * _sigmoid(za))).astype(y_ref.dtype)


def _attention(proj3, col_qa, col_za, mask, c3, ct3, wuk, wuvt, y_all, grp, heads_per_step=16, kt=512):
    B, S, dc = c3.shape
    n_heads, _, dh = wuk.shape
    G = min(heads_per_step, n_heads)
    gw = G * dh
    lk = (grp + 1) * Q_GROUP
    nq = Q_GROUP // Q_BLOCK
    R = G * Q_BLOCK
    kt = math.gcd(kt, lk)
    kern = functools.partial(_attn_kernel, heads=G, kt=kt)
    return pl.pallas_call(
        kern,
        grid=(B, nq, n_heads // G),
        in_specs=[pl.BlockSpec((None, Q_BLOCK, gw), lambda b, q, g: (b, grp * nq + q, col_qa // gw + g)),
                  pl.BlockSpec((None, Q_BLOCK, gw), lambda b, q, g: (b, grp * nq + q, col_za // gw + g)),
                  pl.BlockSpec((None, lk, Q_BLOCK), lambda b, q, g: (b, 0, q)),
                  pl.BlockSpec((None, lk, dc), lambda b, q, g: (b, 0, 0)),
                  pl.BlockSpec((None, dc, lk), lambda b, q, g: (b, 0, 0)),
                  pl.BlockSpec((G, dc, dh), lambda b, q, g: (g, 0, 0)),
                  pl.BlockSpec((G, dh, dc), lambda b, q, g: (g, 0, 0)),
                  pl.BlockSpec(memory_space=pl.ANY)],
        out_specs=pl.BlockSpec((None, Q_BLOCK, gw), lambda b, q, g: (b, grp * nq + q, g)),
        out_shape=jax.ShapeDtypeStruct(y_all.shape, y_all.dtype),
        input_output_aliases={7: 0},
        scratch_shapes=[pltpu.VMEM((dc, R), BF16),
                        pltpu.VMEM((lk, Q_BLOCK), F32),
                        pltpu.VMEM((lk, R), F32)],
        compiler_params=_cparams(("parallel", "parallel", "parallel")),
        name="attention",
    )(proj3, proj3, mask, c3, ct3, wuk, wuvt, y_all)


def _blockdiag_dense(w):
    nblk, blk, _ = w.shape
    per = LANES // blk
    w4 = w.reshape(nblk // per, per, blk, blk)
    eye = jnp.eye(per, dtype=w.dtype)
    dense = jnp.einsum('tgio,gh->tgiho', w4, eye)
    return dense.reshape(nblk // per, LANES, LANES).astype(BF16)


def kernel(x, g_norm, w_in, conv_w, conv_b, w_q_m, w_k_m, w_v_m, b_i, b_f, g_head_m, skip_m, g_ckv, w_uk, w_uv, g_kidx, b_kidx, w_bm, w_ba, w_out, g_final):
    B, S, D = x.shape
    depth = g_norm.shape[0]
    T = B * S
    dm = conv_w.shape[-1]
    mh = b_i.shape[-1]
    ah, dc, adh = w_uk.shape[1:]
    da = ah * adh
    di = g_kidx.shape[-1]
    n_in = w_in.shape[-1]
    n_idx = (n_in - (3 * dm + 2 * mh + da + dc + da + di + 2 * D)) // (di + 1)
    topk = min(IDX_TOPK, S // 4)
    assert 2 * mh + n_idx <= LANES and S % Q_GROUP == 0

    sizes = (dm, dm, dm, mh, mh, da, dc, da, n_idx * di, di, n_idx, D, D)
    offs = [0]
    for s in sizes:
        offs.append(offs[-1] + s)
    (o_xm, o_om, o_zm, o_ip, o_fp, o_qa, o_ckv, o_za, o_qi, o_ki, o_wi, o_gm, o_ga) = offs[:-1]

    order = [(o_xm, dm), (o_om, dm), (o_zm, dm), (o_qa, da), (o_za, da), (o_qi, n_idx * di),
             (o_gm, D), (o_ga, D), (o_ckv, dc), (o_ki, di)]
    cols, pos = {}, 0
    for off, width in order:
        cols[off] = pos
        pos += width
    c_small = pos
    pos += LANES
    tn = 1024
    n_pad = -(-pos // tn) * tn

    w_bm_rows = w_bm.reshape(depth * dm, D)
    w_ba_rows = w_ba.reshape(depth * da, D)
    w_out_rows = w_out.reshape(depth * D, D)

    x2 = x.reshape(T, D)
    for l in range(depth):
        w = w_in[l]
        parts = [w[:, off:off + width].astype(BF16) for off, width in order]
        parts += [w[:, o_ip:o_ip + mh].astype(BF16), w[:, o_fp:o_fp + mh].astype(BF16),
                  w[:, o_wi:o_wi + n_idx].astype(BF16),
                  jnp.zeros((D, n_pad - c_small - 2 * mh - n_idx), BF16)]
        w_cat = jnp.concatenate(parts, axis=1)

        xn = _rmsnorm(x2, g_norm[l], BF16)
        proj = _matmul(xn, w_cat, F32, tn=tn)
        proj3 = proj.reshape(B, S, n_pad)

        gate_bias = jnp.concatenate([b_i[l], b_f[l], jnp.zeros((LANES - 2 * mh,), F32)]).reshape(1, LANES)
        y_m = _mlstm(proj3, cols[o_xm], cols[o_om], cols[o_zm], c_small, conv_w[l], conv_b[l],
                     _blockdiag_dense(w_q_m[l]), _blockdiag_dense(w_k_m[l]), _blockdiag_dense(w_v_m[l]),
                     gate_bias, g_head_m[l], skip_m[l], mh)

        c3, ct3, kidx3 = _dsa_prep(proj3, cols[o_ckv], cols[o_ki], g_ckv[l], g_kidx[l], b_kidx[l])
        wuk = w_uk[l].astype(BF16)
        wuvt = jnp.swapaxes(w_uv[l], 1, 2).astype(BF16)
        y_a = jnp.zeros((B, S, da), BF16)
        for grp in range(S // Q_GROUP):
            mask = _indexer(proj3, cols[o_qi], c_small, kidx3, grp, n_idx, 2 * mh, topk)
            y_a = _attention(proj3, cols[o_qa], cols[o_za], mask, c3, ct3, wuk, wuvt, y_a, grp)

        merged = _merge(y_m.reshape(T, dm), y_a.reshape(T, da), _to_bf16(w_bm_rows, l, dm),
                        _to_bf16(w_ba_rows, l, da), proj, cols[o_gm], cols[o_ga])
        x2 = _out_proj(merged, _to_bf16(w_out_rows, l, D), x2)
    out = _rmsnorm(x2, g_final, F32)
    return out.reshape(B, S, D)
```

```python
import functools
import math

import jax
import jax.numpy as jnp
from jax import lax
from jax.experimental import pallas as pl
from jax.experimental.pallas import tpu as pltpu

EPS = 1e-6
M_QKV_BLOCK = 4
M_CONV = 4
IDX_TOPK = 256
Q_BLOCK = 128
Q_GROUP = 256
LANES = 128
SUBLANES = 8
VMEM_LIMIT = 56 * 1024 * 1024

F32 = jnp.float32
BF16 = jnp.bfloat16
INT_MIN = -2 ** 31


def _cparams(sem):
    return pltpu.CompilerParams(dimension_semantics=sem, vmem_limit_bytes=VMEM_LIMIT)


def _rmsnorm_kernel(x_ref, g_ref, o_ref):
    x = x_ref[...]
    y = x * lax.rsqrt(jnp.mean(x * x, axis=-1, keepdims=True) + EPS)
    o_ref[...] = (y * g_ref[...]).astype(o_ref.dtype)


def _rmsnorm(x2d, g, out_dtype, tm=256):
    T, D = x2d.shape
    tm = min(tm, T)
    return pl.pallas_call(
        _rmsnorm_kernel,
        grid=(T // tm,),
        in_specs=[pl.BlockSpec((tm, D), lambda i: (i, 0)),
                  pl.BlockSpec((1, D), lambda i: (0, 0))],
        out_specs=pl.BlockSpec((tm, D), lambda i: (i, 0)),
        out_shape=jax.ShapeDtypeStruct((T, D), out_dtype),
        compiler_params=_cparams(("parallel",)),
        name="rmsnorm",
    )(x2d, g.reshape(1, D))


def _cast_kernel(x_ref, o_ref):
    o_ref[...] = x_ref[...].astype(o_ref.dtype)


def _to_bf16(w_stack, layer, rows, tm=512):
    N = w_stack.shape[1]
    tm = min(tm, rows)
    nb = rows // tm
    return pl.pallas_call(
        _cast_kernel,
        grid=(nb,),
        in_specs=[pl.BlockSpec((tm, N), lambda i: (layer * nb + i, 0))],
        out_specs=pl.BlockSpec((tm, N), lambda i: (i, 0)),
        out_shape=jax.ShapeDtypeStruct((rows, N), BF16),
        compiler_params=_cparams(("parallel",)),
        name="cast_bf16",
    )(w_stack)


def _mm_kernel(a_ref, w_ref, o_ref):
    o_ref[...] = jnp.dot(a_ref[...], w_ref[...], preferred_element_type=F32).astype(o_ref.dtype)


def _matmul(a, w, out_dtype, tm=1024, tn=512):
    M, K = a.shape
    N = w.shape[1]
    tm, tn = min(tm, M), min(tn, N)
    return pl.pallas_call(
        _mm_kernel,
        grid=(M // tm, N // tn),
        in_specs=[pl.BlockSpec((tm, K), lambda i, j: (i, 0)),
                  pl.BlockSpec((K, tn), lambda i, j: (0, j))],
        out_specs=pl.BlockSpec((tm, tn), lambda i, j: (i, j)),
        out_shape=jax.ShapeDtypeStruct((M, N), out_dtype),
        compiler_params=_cparams(("parallel", "parallel")),
        name="in_proj",
    )(a, w)


def _merge_kernel(ym_ref, ya_ref, wbm_ref, wba_ref, gm_ref, ga_ref, o_ref):
    half = o_ref.shape[1] // 2
    for c in range(2):
        cs = slice(c * half, (c + 1) * half)
        pm = jnp.dot(ym_ref[...], wbm_ref[:, cs], preferred_element_type=F32)
        pa = jnp.dot(ya_ref[...], wba_ref[:, cs], preferred_element_type=F32)
        o = _sigmoid(gm_ref[:, cs]) * pm + _sigmoid(ga_ref[:, cs]) * pa
        o_ref[:, cs] = o.astype(o_ref.dtype)


def _merge(ym, ya, wbm, wba, proj, gm_col, ga_col, tm=512, tn=512):
    M, K = ym.shape
    N = wbm.shape[1]
    tm, tn = min(tm, M), min(tn, N)
    gm_blk, ga_blk = gm_col // tn, ga_col // tn
    return pl.pallas_call(
        _merge_kernel,
        grid=(M // tm, N // tn),
        in_specs=[pl.BlockSpec((tm, K), lambda i, j: (i, 0)),
                  pl.BlockSpec((tm, K), lambda i, j: (i, 0)),
                  pl.BlockSpec((K, tn), lambda i, j: (0, j)),
                  pl.BlockSpec((K, tn), lambda i, j: (0, j)),
                  pl.BlockSpec((tm, tn), lambda i, j: (i, gm_blk + j)),
                  pl.BlockSpec((tm, tn), lambda i, j: (i, ga_blk + j))],
        out_specs=pl.BlockSpec((tm, tn), lambda i, j: (i, j)),
        out_shape=jax.ShapeDtypeStruct((M, N), BF16),
        compiler_params=_cparams(("parallel", "parallel")),
        name="merge",
    )(ym, ya, wbm, wba, proj, proj)


def _out_kernel(a_ref, w_ref, x_ref, o_ref):
    o_ref[...] = x_ref[...] + jnp.dot(a_ref[...], w_ref[...], preferred_element_type=F32)


def _out_proj(a, w, x2d, tm=1024, tn=512):
    M, K = a.shape
    N = w.shape[1]
    tm, tn = min(tm, M), min(tn, N)
    return pl.pallas_call(
        _out_kernel,
        grid=(M // tm, N // tn),
        in_specs=[pl.BlockSpec((tm, K), lambda i, j: (i, 0)),
                  pl.BlockSpec((K, tn), lambda i, j: (0, j)),
                  pl.BlockSpec((tm, tn), lambda i, j: (i, j))],
        out_specs=pl.BlockSpec((tm, tn), lambda i, j: (i, j)),
        out_shape=jax.ShapeDtypeStruct((M, N), F32),
        compiler_params=_cparams(("parallel", "parallel")),
        name="out_proj",
    )(a, w, x2d)


def _sigmoid(x):
    return 0.5 * jnp.tanh(0.5 * x) + 0.5


def _log_sigmoid(x):
    return -(jnp.maximum(-x, 0.0) + jnp.log1p(jnp.exp(-jnp.abs(x))))


def _mlstm_kernel(xm_ref, om_ref, zm_ref, sm_ref, cw_ref, cb_ref, wq_ref, wk_ref, wv_ref,
                  gb_ref, gh_ref, sk_ref, y_ref,
                  xbuf, xc_s, q_s, k_s, v_s, c_s, n_s, m_s, *, n_heads, rows, chunk):
    h = pl.program_id(1)
    j = pl.program_id(2)
    dh = xc_s.shape[1]
    L = chunk
    pad = SUBLANES

    @pl.when(j == 0)
    def _():
        xbuf[0:pad, :] = jnp.zeros((pad, dh), F32)
        c_s[...] = jnp.zeros_like(c_s)
        n_s[...] = jnp.zeros_like(n_s)
        m_s[...] = jnp.zeros_like(m_s)

    xblk = xm_ref[...]
    xbuf[pad:pad + rows, :] = xblk
    acc = jnp.zeros((rows, dh), F32) + cb_ref[...]
    for k in range(M_CONV):
        acc = acc + cw_ref[k:k + 1, :] * xbuf[pl.ds(pad - (M_CONV - 1) + k, rows), :]
    xc = acc * _sigmoid(acc)
    xc_s[...] = xc
    xbuf[0:pad, :] = xblk[rows - pad:rows, :]

    scale = dh ** -0.5
    for t in range(dh // LANES):
        sl = slice(t * LANES, (t + 1) * LANES)
        xct = xc[:, sl].astype(BF16)
        xmt = xblk[:, sl].astype(BF16)
        q_s[:, sl] = jnp.dot(xct, wq_ref[t], preferred_element_type=F32).astype(BF16)
        k_s[:, sl] = jnp.dot(xct, wk_ref[t], preferred_element_type=F32) * scale
        v_s[:, sl] = jnp.dot(xmt, wv_ref[t], preferred_element_type=F32).astype(BF16)

    lane = lax.broadcasted_iota(jnp.int32, (L, LANES), 1)
    row_i = lax.broadcasted_iota(jnp.int32, (L, L), 0)
    col_i = lax.broadcasted_iota(jnp.int32, (L, L), 1)
    tril = col_i <= row_i
    eye = col_i == row_i

    def to_row(col):
        return jnp.sum(jnp.where(eye, col, 0.0), axis=0, keepdims=True)

    def chunk(c, carry):
        r0 = pl.multiple_of(c * L, L)
        rs = pl.ds(r0, L)
        g = sm_ref[rs, :] + gb_ref[...]
        i_col = jnp.sum(jnp.where(lane == h, g, 0.0), axis=1, keepdims=True)
        f_col = _log_sigmoid(jnp.sum(jnp.where(lane == n_heads + h, g, 0.0), axis=1, keepdims=True))
        i_row = to_row(i_col)
        f_row = to_row(f_col)
        b_col = jnp.sum(jnp.where(tril, f_row, 0.0), axis=1, keepdims=True)
        b_row = jnp.sum(jnp.where(col_i >= row_i, f_col, 0.0), axis=0, keepdims=True)
        m_prev = m_s[...]
        dmat = jnp.where(tril, b_col - b_row + i_row, -jnp.inf)
        m_inter = b_col + m_prev
        m_t = jnp.maximum(m_inter, jnp.max(dmat, axis=1, keepdims=True))
        qc = q_s[rs, :]
        kc = k_s[rs, :]
        vc = v_s[rs, :]
        s_qk = lax.dot_general(qc, kc.astype(BF16), (((1,), (1,)), ((), ())),
                               preferred_element_type=F32)
        a = s_qk * jnp.exp(dmat - m_t)
        inter = jnp.exp(m_inter - m_t)
        c_prev = c_s[...]
        num = (jnp.dot(a.astype(BF16), vc, preferred_element_type=F32)
               + inter * jnp.dot(qc, c_prev.astype(BF16), preferred_element_type=F32))
        qn = jnp.sum(qc.astype(F32) * n_s[...], axis=1, keepdims=True)
        den = jnp.sum(a, axis=1, keepdims=True) + inter * qn
        hh = num / jnp.maximum(jnp.abs(den), jnp.exp(-m_t))

        mu = jnp.mean(hh, axis=1, keepdims=True)
        hc = hh - mu
        hn = hc * lax.rsqrt(jnp.mean(hc * hc, axis=1, keepdims=True) + EPS) * gh_ref[...]
        hn = _sigmoid(om_ref[rs, :]) * hn
        zm = zm_ref[rs, :]
        y = (hn + sk_ref[...] * xc_s[rs, :]) * (zm * _sigmoid(zm))
        y_ref[rs, :] = y.astype(y_ref.dtype)

        b_last = b_col[L - 1:L, :]
        g_col = b_last - b_col + i_col
        m_new = jnp.maximum(b_last + m_prev, jnp.max(g_col, axis=0, keepdims=True))
        w_col = jnp.exp(g_col - m_new)
        decay = jnp.exp(b_last + m_prev - m_new)
        kw = kc * w_col
        c_s[...] = decay * c_prev + lax.dot_general(
            kw.astype(BF16), vc, (((0,), (0,)), ((), ())), preferred_element_type=F32)
        n_s[...] = decay * n_s[...] + jnp.sum(kw, axis=0, keepdims=True)
        m_s[...] = m_new
        return carry

    lax.fori_loop(0, rows // L, chunk, 0, unroll=True)


def _mlstm(proj3, col_xm, col_om, col_zm, col_small, conv_w, conv_b, wq_d, wk_d, wv_d,
           gate_bias, g_head, skip, n_heads, rows=512, chunk=256):
    B, S, _ = proj3.shape
    dm = conv_w.shape[1]
    dh = dm // n_heads
    rows = min(rows, S)
    chunk = min(chunk, rows)
    assert rows % chunk == 0 and S % rows == 0
    nt = dh // LANES
    bx, bo, bz, bs = col_xm // dh, col_om // dh, col_zm // dh, col_small // LANES
    seq = lambda base: pl.BlockSpec((None, rows, dh), lambda b, h, j: (b, j, base + h))
    vec = pl.BlockSpec((1, dh), lambda b, h, j: (0, h))
    wsp = pl.BlockSpec((nt, LANES, LANES), lambda b, h, j: (h, 0, 0))
    kern = functools.partial(_mlstm_kernel, n_heads=n_heads, rows=rows, chunk=chunk)
    return pl.pallas_call(
        kern,
        grid=(B, n_heads, S // rows),
        in_specs=[seq(bx), seq(bo), seq(bz),
                  pl.BlockSpec((None, rows, LANES), lambda b, h, j: (b, j, bs)),
                  pl.BlockSpec((M_CONV, dh), lambda b, h, j: (0, h)),
                  vec, wsp, wsp, wsp,
                  pl.BlockSpec((1, LANES), lambda b, h, j: (0, 0)),
                  vec, vec],
        out_specs=pl.BlockSpec((None, rows, dh), lambda b, h, j: (b, j, h)),
        out_shape=jax.ShapeDtypeStruct((B, S, dm), BF16),
        scratch_shapes=[pltpu.VMEM((rows + SUBLANES, dh), F32),
                        pltpu.VMEM((rows, dh), F32),
                        pltpu.VMEM((rows, dh), BF16),
                        pltpu.VMEM((rows, dh), F32),
                        pltpu.VMEM((rows, dh), BF16),
                        pltpu.VMEM((dh, dh), F32),
                        pltpu.VMEM((1, dh), F32),
                        pltpu.VMEM((1, 1), F32)],
        compiler_params=_cparams(("parallel", "parallel", "arbitrary")),
        name="mlstm",
    )(proj3, proj3, proj3, proj3, conv_w, conv_b.reshape(1, dm), wq_d, wk_d, wv_d,
      gate_bias, g_head.reshape(1, dm), skip.reshape(1, dm))


def _dsa_prep_kernel(ckv_ref, ki_ref, gc_ref, gk_ref, bk_ref, c_ref, ct_ref, kidx_ref):
    x = ckv_ref[...]
    c = x * lax.rsqrt(jnp.mean(x * x, axis=-1, keepdims=True) + EPS) * gc_ref[...]
    c_ref[...] = c.astype(c_ref.dtype)
    ct_ref[...] = c.T.astype(ct_ref.dtype)
    k = ki_ref[...]
    mu = jnp.mean(k, axis=-1, keepdims=True)
    kc = k - mu
    y = kc * lax.rsqrt(jnp.mean(kc * kc, axis=-1, keepdims=True) + EPS)
    kidx_ref[...] = (y * gk_ref[...] + bk_ref[...]).astype(kidx_ref.dtype)


def _dsa_prep(proj3, col_ckv, col_ki, g_ckv, g_kidx, b_kidx, tm=512):
    B, S, _ = proj3.shape
    dc, di = g_ckv.shape[0], g_kidx.shape[0]
    tm = min(tm, S)
    return pl.pallas_call(
        _dsa_prep_kernel,
        grid=(B, S // tm),
        in_specs=[pl.BlockSpec((None, tm, dc), lambda b, i: (b, i, col_ckv // dc)),
                  pl.BlockSpec((None, tm, di), lambda b, i: (b, i, col_ki // di)),
                  pl.BlockSpec((1, dc), lambda b, i: (0, 0)),
                  pl.BlockSpec((1, di), lambda b, i: (0, 0)),
                  pl.BlockSpec((1, di), lambda b, i: (0, 0))],
        out_specs=[pl.BlockSpec((None, tm, dc), lambda b, i: (b, i, 0)),
                   pl.BlockSpec((None, dc, tm), lambda b, i: (b, 0, i)),
                   pl.BlockSpec((None, tm, di), lambda b, i: (b, i, 0))],
        out_shape=[jax.ShapeDtypeStruct((B, S, dc), BF16),
                   jax.ShapeDtypeStruct((B, dc, S), BF16),
                   jax.ShapeDtypeStruct((B, S, di), BF16)],
        compiler_params=_cparams(("parallel", "parallel")),
        name="dsa_prep",
    )(proj3, proj3, g_ckv.reshape(1, dc), g_kidx.reshape(1, di), b_kidx.reshape(1, di))


def _indexer_kernel(qi_ref, sm_ref, kidx_ref, mask_ref, score_s, key_s, wt_s,
                    *, n_idx, wi_lane, topk, q0):
    lk, di = kidx_ref.shape
    qg = qi_ref.shape[0]
    kf = float(topk)
    wt_s[...] = sm_ref[...].T * ((n_idx ** -0.5) * (di ** -0.5))
    kidx = kidx_ref[...]
    score_s[...] = jnp.zeros_like(score_s)

    def head(hh, carry):
        c0 = pl.multiple_of(hh * di, di)
        qh = qi_ref[:, pl.ds(c0, di)].astype(BF16)
        logits = lax.dot_general(kidx, qh, (((1,), (1,)), ((), ())), preferred_element_type=F32)
        w = wt_s[pl.ds(wi_lane + hh, 1), :]
        score_s[...] += jnp.maximum(logits, 0.0) * w
        return carry

    lax.fori_loop(0, n_idx, head, 0, unroll=2)

    bits = pltpu.bitcast(score_s[...], jnp.int32)
    key = jnp.where(bits >= 0, bits, bits ^ jnp.int32(0x7FFFFFFF))
    s_pos = lax.broadcasted_iota(jnp.int32, (lk, qg), 0)
    t_pos = q0 + lax.broadcasted_iota(jnp.int32, (lk, qg), 1)
    causal = s_pos <= t_pos
    key_s[...] = jnp.where(causal, key, jnp.int32(INT_MIN))

    def count(pred):
        return jnp.sum(jnp.where(pred, 1.0, 0.0), axis=0, keepdims=True)

    thr = jnp.where(count(key_s[...] >= 0) >= kf, jnp.int32(0), jnp.int32(INT_MIN))

    def bit_step(it, thr):
        cand = thr + jnp.left_shift(jnp.int32(1), 30 - it)
        return jnp.where(count(key_s[...] >= cand) >= kf, cand, thr)

    thr = lax.fori_loop(0, 31, bit_step, thr)
    key = key_s[...]
    ge = key >= thr
    mask_ref[...] = jnp.where(jnp.logical_and(ge, causal), 1, 0).astype(mask_ref.dtype)

    excess = jnp.logical_and(count(ge) > kf, thr > jnp.int32(INT_MIN))

    @pl.when(jnp.max(jnp.where(excess, 1.0, 0.0)) > 0.0)
    def _():
        gt = key > thr
        tie = key == thr
        need = kf - count(gt)
        nbits = max(1, (lk - 1).bit_length())

        def idx_step(it, x):
            cand = x + jnp.left_shift(jnp.int32(1), nbits - 1 - it)
            c = count(jnp.logical_and(tie, s_pos < cand))
            return jnp.where(c < need, cand, x)

        x = lax.fori_loop(0, nbits, idx_step, jnp.zeros((1, qg), jnp.int32))
        sel = jnp.logical_or(gt, jnp.logical_and(tie, s_pos <= x))
        mask_ref[...] = jnp.where(jnp.logical_and(sel, causal), 1, 0).astype(mask_ref.dtype)


def _indexer(proj3, col_qi, col_small, kidx3, grp, n_idx, wi_lane, topk):
    B, S, di = kidx3.shape
    wq = n_idx * di
    lk = (grp + 1) * Q_GROUP
    kern = functools.partial(_indexer_kernel, n_idx=n_idx, wi_lane=wi_lane, topk=topk,
                             q0=grp * Q_GROUP)
    return pl.pallas_call(
        kern,
        grid=(B,),
        in_specs=[pl.BlockSpec((None, Q_GROUP, wq), lambda b: (b, grp, col_qi // wq)),
                  pl.BlockSpec((None, Q_GROUP, LANES), lambda b: (b, grp, col_small // LANES)),
                  pl.BlockSpec((None, lk, di), lambda b: (b, 0, 0))],
        out_specs=pl.BlockSpec((None, lk, Q_GROUP), lambda b: (b, 0, 0)),
        out_shape=jax.ShapeDtypeStruct((B, lk, Q_GROUP), jnp.int8),
        scratch_shapes=[pltpu.VMEM((lk, Q_GROUP), F32),
                        pltpu.VMEM((lk, Q_GROUP), jnp.int32),
                        pltpu.VMEM((LANES, Q_GROUP), F32)],
        compiler_params=_cparams(("parallel",)),
        name="indexer",
    )(proj3, proj3, kidx3)


def _attn_kernel(qa_ref, za_ref, mask_ref, c_ref, ct_ref, wuk_ref, wuvt_ref, y_all_ref, y_ref,
                 qlat_s, bias_s, s_s, *, heads, kt):
    del y_all_ref
    lk, dc = c_ref.shape
    dh = wuk_ref.shape[2]
    qb = qa_ref.shape[0]
    R = heads * qb
    assert lk % kt == 0 and kt % SUBLANES == 0
    scale = (dh ** -0.5) * math.log2(math.e)
    nt = (((1,), (1,)), ((), ()))

    for g in range(heads):
        qh = qa_ref[:, g * dh:(g + 1) * dh].astype(BF16)
        ql = lax.dot_general(wuk_ref[g], qh, nt, preferred_element_type=F32)
        qlat_s[:, g * qb:(g + 1) * qb] = (ql * scale).astype(BF16)
    bias_s[...] = jnp.where(mask_ref[...].astype(jnp.int32) != 0, 0.0, -jnp.inf)

    m8 = jnp.full((SUBLANES, R), -jnp.inf, F32)
    for i in range(lk // kt):
        ks = slice(i * kt, (i + 1) * kt)
        st = jnp.dot(c_ref[ks, :], qlat_s[...], preferred_element_type=F32)
        b = bias_s[ks, :]
        tops = []
        for g in range(heads):
            blk = st[:, g * qb:(g + 1) * qb] + b
            s_s[ks, g * qb:(g + 1) * qb] = blk
            tops.append(jnp.max(blk.reshape(kt // SUBLANES, SUBLANES, qb), axis=0))
        m8 = jnp.maximum(m8, jnp.concatenate(tops, axis=1))
    m = jnp.max(m8, axis=0, keepdims=True)

    l8 = jnp.zeros((SUBLANES, R), F32)
    olat = jnp.zeros((dc, R), F32)
    for i in range(lk // kt):
        ks = slice(i * kt, (i + 1) * kt)
        p = jnp.exp2(s_s[ks, :] - m)
        l8 = l8 + jnp.sum(p.reshape(kt // SUBLANES, SUBLANES, R), axis=0)
        olat = olat + jnp.dot(ct_ref[:, ks], p.astype(BF16), preferred_element_type=F32)
    l = jnp.sum(l8, axis=0, keepdims=True)
    olat = (olat * (1.0 / l)).astype(BF16)
    for g in range(heads):
        ot = jnp.dot(wuvt_ref[g], olat[:, g * qb:(g + 1) * qb], preferred_element_type=F32)
        za = za_ref[:, g * dh:(g + 1) * dh]
        y_ref[:, g * dh:(g + 1) * dh] = (ot.T * (za * _sigmoid(za))).astype(y_ref.dtype)


def _attention(proj3, col_qa, col_za, mask, c3, ct3, wuk, wuvt, y_all, grp, heads_per_step=16, kt=512):
    B, S, dc = c3.shape
    n_heads, _, dh = wuk.shape
    G = min(heads_per_step, n_heads)
    gw = G * dh
    lk = (grp + 1) * Q_GROUP
    nq = Q_GROUP // Q_BLOCK
    R = G * Q_BLOCK
    kt = math.gcd(kt, lk)
    kern = functools.partial(_attn_kernel, heads=G, kt=kt)
    return pl.pallas_call(
        kern,
        grid=(B, nq, n_heads // G),
        in_specs=[pl.BlockSpec((None, Q_BLOCK, gw), lambda b, q, g: (b, grp * nq + q, col_qa // gw + g)),
                  pl.BlockSpec((None, Q_BLOCK, gw), lambda b, q, g: (b, grp * nq + q, col_za // gw + g)),
                  pl.BlockSpec((None, lk, Q_BLOCK), lambda b, q, g: (b, 0, q)),
                  pl.BlockSpec((None, lk, dc), lambda b, q, g: (b, 0, 0)),
                  pl.BlockSpec((None, dc, lk), lambda b, q, g: (b, 0, 0)),
                  pl.BlockSpec((G, dc, dh), lambda b, q, g: (g, 0, 0)),
                  pl.BlockSpec((G, dh, dc), lambda b, q, g: (g, 0, 0)),
                  pl.BlockSpec(memory_space=pl.ANY)],
        out_specs=pl.BlockSpec((None, Q_BLOCK, gw), lambda b, q, g: (b, grp * nq + q, g)),
        out_shape=jax.ShapeDtypeStruct(y_all.shape, y_all.dtype),
        input_output_aliases={7: 0},
        scratch_shapes=[pltpu.VMEM((dc, R), BF16),
                        pltpu.VMEM((lk, Q_BLOCK), F32),
                        pltpu.VMEM((lk, R), F32)],
        compiler_params=_cparams(("parallel", "parallel", "parallel")),
        name="attention",
    )(proj3, proj3, mask, c3, ct3, wuk, wuvt, y_all)


def _blockdiag_dense(w):
    nblk, blk, _ = w.shape
    per = LANES // blk
    w4 = w.reshape(nblk // per, per, blk, blk)
    eye = jnp.eye(per, dtype=w.dtype)
    dense = jnp.einsum('tgio,gh->tgiho', w4, eye)
    return dense.reshape(nblk // per, LANES, LANES).astype(BF16)


def kernel(x, g_norm, w_in, conv_w, conv_b, w_q_m, w_k_m, w_v_m, b_i, b_f, g_head_m, skip_m, g_ckv, w_uk, w_uv, g_kidx, b_kidx, w_bm, w_ba, w_out, g_final):
    B, S, D = x.shape
    depth = g_norm.shape[0]
    T = B * S
    dm = conv_w.shape[-1]
    mh = b_i.shape[-1]
    ah, dc, adh = w_uk.shape[1:]
    da = ah * adh
    di = g_kidx.shape[-1]
    n_in = w_in.shape[-1]
    n_idx = (n_in - (3 * dm + 2 * mh + da + dc + da + di + 2 * D)) // (di + 1)
    topk = min(IDX_TOPK, S // 4)
    assert 2 * mh + n_idx <= LANES and S % Q_GROUP == 0

    sizes = (dm, dm, dm, mh, mh, da, dc, da, n_idx * di, di, n_idx, D, D)
    offs = [0]
    for s in sizes:
        offs.append(offs[-1] + s)
    (o_xm, o_om, o_zm, o_ip, o_fp, o_qa, o_ckv, o_za, o_qi, o_ki, o_wi, o_gm, o_ga) = offs[:-1]

    order = [(o_xm, dm), (o_om, dm), (o_zm, dm), (o_qa, da), (o_za, da), (o_qi, n_idx * di),
             (o_gm, D), (o_ga, D), (o_ckv, dc), (o_ki, di)]
    cols, pos = {}, 0
    for off, width in order:
        cols[off] = pos
        pos += width
    c_small = pos
    pos += LANES
    tn = 1024
    n_pad = -(-pos // tn) * tn

    w_bm_rows = w_bm.reshape(depth * dm, D)
    w_ba_rows = w_ba.reshape(depth * da, D)
    w_out_rows = w_out.reshape(depth * D, D)

    x2 = x.reshape(T, D)
    for l in range(depth):
        w = w_in[l]
        parts = [w[:, off:off + width].astype(BF16) for off, width in order]
        parts += [w[:, o_ip:o_ip + mh].astype(BF16), w[:, o_fp:o_fp + mh].astype(BF16),
                  w[:, o_wi:o_wi + n_idx].astype(BF16),
                  jnp.zeros((D, n_pad - c_small - 2 * mh - n_idx), BF16)]
        w_cat = jnp.concatenate(parts, axis=1)

        xn = _rmsnorm(x2, g_norm[l], BF16)
        proj = _matmul(xn, w_cat, F32, tn=tn)
        proj3 = proj.reshape(B, S, n_pad)

        gate_bias = jnp.concatenate([b_i[l], b_f[l], jnp.zeros((LANES - 2 * mh,), F32)]).reshape(1, LANES)
        y_m = _mlstm(proj3, cols[o_xm], cols[o_om], cols[o_zm], c_small, conv_w[l], conv_b[l],
                     _blockdiag_dense(w_q_m[l]), _blockdiag_dense(w_k_m[l]), _blockdiag_dense(w_v_m[l]),
                     gate_bias, g_head_m[l], skip_m[l], mh)

        c3, ct3, kidx3 = _dsa_prep(proj3, cols[o_ckv], cols[o_ki], g_ckv[l], g_kidx[l], b_kidx[l])
        wuk = w_uk[l].astype(BF16)
        wuvt = jnp.swapaxes(w_uv[l], 1, 2).astype(BF16)
        y_a = jnp.zeros((B, S, da), BF16)
        for grp in range(S // Q_GROUP):
            mask = _indexer(proj3, cols[o_qi], c_small, kidx3, grp, n_idx, 2 * mh, topk)
            y_a = _attention(proj3, cols[o_qa], cols[o_za], mask, c3, ct3, wuk, wuvt, y_a, grp)

        merged = _merge(y_m.reshape(T, dm), y_a.reshape(T, da), _to_bf16(w_bm_rows, l, dm),
                        _to_bf16(w_ba_rows, l, da), proj, cols[o_gm], cols[o_ga])
        x2 = _out_proj(merged, _to_bf16(w_out_rows, l, D), x2)
    out = _rmsnorm(x2, g_final, F32)
    return out.reshape(B, S, D)
```
